```python
import jax, jax.numpy as jnp
from jax import lax
import numpy as np

D_MODEL = 1024
BATCH = 16
SEQ = 2048
DEPTH = 1

D_RWKV = 512
HEAD_SIZE = 64
N_RWKV_HEADS = D_RWKV // HEAD_SIZE
D_CONV = D_MODEL - D_RWKV
CONV_WIDTH = 31
D_DECAY_LORA = 64
D_AAA_LORA = 64
D_GATE_LORA = 128
D_FF = 4 * D_MODEL
RMS_EPS = 1e-6
GN_EPS = 64e-5
LN_EPS = 1e-5
D_RWKV_IN = 3 * D_RWKV + D_DECAY_LORA + D_AAA_LORA + D_GATE_LORA
D_IN = D_RWKV_IN + 2 * D_CONV

kernel_name = "hymba_rwkv7_conformer_conv_hybrid"


def rmsnorm(x, g):
    xf = x.astype(jnp.float32)
    y = xf * lax.rsqrt(jnp.mean(xf * xf, axis=-1, keepdims=True) + RMS_EPS)
    return (y * g.astype(jnp.float32)).astype(x.dtype)


def token_shift(p):
    return jnp.pad(p[:, :-1], ((0, 0), (1, 0), (0, 0)))


def rwkv7_recurrence(r, decay, k, v, a_vec, b_vec):
    B, T, H, N = r.shape
    xs = tuple(jnp.swapaxes(t, 0, 1) for t in (r, decay, k, v, a_vec, b_vec))

    def step(S, inp):
        r_t, w_t, k_t, v_t, a_t, b_t = inp
        sa = jnp.einsum('bhvk,bhk->bhv', S, a_t)
        S = S * w_t[:, :, None, :] + sa[..., :, None] * b_t[:, :, None, :] + v_t[..., :, None] * k_t[:, :, None, :]
        y = jnp.einsum('bhvk,bhk->bhv', S, r_t)
        return S, y

    S0 = jnp.zeros((B, H, N, N), jnp.float32)
    _, ys = lax.scan(step, S0, xs)
    return jnp.swapaxes(ys, 0, 1)


def rwkv7_group(p, w0, w_decay_up, a0, w_aaa_up, w_gate_up, k_k, k_a, r_k, ln_x_g, ln_x_b):
    B, T, _ = p.shape
    o = 0
    r = p[..., o:o + D_RWKV]; o += D_RWKV
    k = p[..., o:o + D_RWKV]; o += D_RWKV
    v = p[..., o:o + D_RWKV]; o += D_RWKV
    w_down = p[..., o:o + D_DECAY_LORA]; o += D_DECAY_LORA
    a_down = p[..., o:o + D_AAA_LORA]; o += D_AAA_LORA
    g_down = p[..., o:o + D_GATE_LORA]

    w_raw = -jax.nn.softplus(-(w0 + jnp.tanh(w_down) @ w_decay_up).astype(jnp.float32)) - 0.5
    decay = jnp.exp(-jnp.exp(w_raw))
    a = jax.nn.sigmoid(a0 + a_down @ w_aaa_up)
    g = jax.nn.sigmoid(g_down) @ w_gate_up

    heads = lambda t: t.reshape(B, T, N_RWKV_HEADS, HEAD_SIZE).astype(jnp.float32)
    kk = heads(k * k_k)
    kk = kk / jnp.maximum(jnp.linalg.norm(kk, axis=-1, keepdims=True), 1e-12)
    k = k * (1.0 + (a - 1.0) * k_a)

    rh, kh, vh, ah = heads(r), heads(k), heads(v), heads(a)
    y = rwkv7_recurrence(rh, heads(decay), kh, vh, -kk, kk * ah)

    mu = jnp.mean(y, axis=-1, keepdims=True)
    var = jnp.mean(jnp.square(y - mu), axis=-1, keepdims=True)
    y = (y - mu) * lax.rsqrt(var + GN_EPS)
    y = y.reshape(B, T, D_RWKV) * ln_x_g + ln_x_b
    bonus = jnp.sum(rh * kh * heads(jnp.broadcast_to(r_k, r.shape)), axis=-1, keepdims=True) * vh
    y = y + bonus.reshape(B, T, D_RWKV)
    return (y * g).astype(p.dtype)


def conformer_conv_group(u, conv_w, conv_b, conv_ln_g, conv_ln_b):
    h = u[..., :D_CONV] * jax.nn.sigmoid(u[..., D_CONV:])
    h = lax.conv_general_dilated(
        h, conv_w[:, None, :].astype(h.dtype), window_strides=(1,),
        padding=[(CONV_WIDTH - 1, 0)],
        dimension_numbers=('NWC', 'WIO', 'NWC'),
        feature_group_count=D_CONV) + conv_b
    hf = h.astype(jnp.float32)
    mu = jnp.mean(hf, axis=-1, keepdims=True)
    var = jnp.mean(jnp.square(hf - mu), axis=-1, keepdims=True)
    hn = (hf - mu) * lax.rsqrt(var + LN_EPS) * conv_ln_g + conv_ln_b
    return jax.nn.silu(hn).astype(u.dtype)


def setup_inputs(seed: int = 0) -> dict:
    key = jax.random.key(seed)
    ks = jax.random.split(key, 24)
    f32 = jnp.float32
    nrm = lambda k, shape, s: jax.random.normal(k, shape, f32) * s
    L = DEPTH
    decay_base = jnp.linspace(-6.5, -1.5, D_RWKV, dtype=f32)
    return {
        "x": jax.random.normal(ks[0], (BATCH, SEQ, D_MODEL), f32),
        "norm_mix_g": 1.0 + nrm(ks[1], (L, D_MODEL), 0.02),
        "w_in": nrm(ks[2], (L, D_MODEL, D_IN), D_MODEL ** -0.5),
        "shift_mu": jax.random.uniform(ks[3], (L, D_RWKV_IN), f32, 0.2, 0.8),
        "w0": decay_base[None, :] + nrm(ks[4], (L, D_RWKV), 0.1),
        "w_decay_up": nrm(ks[5], (L, D_DECAY_LORA, D_RWKV), 0.3 * D_DECAY_LORA ** -0.5),
        "a0": nrm(ks[6], (L, D_RWKV), 0.1),
        "w_aaa_up": nrm(ks[7], (L, D_AAA_LORA, D_RWKV), 0.3 * D_AAA_LORA ** -0.5),
        "w_gate_up": nrm(ks[8], (L, D_GATE_LORA, D_RWKV), D_GATE_LORA ** -0.5),
        "k_k": 0.85 + nrm(ks[9], (L, D_RWKV), 0.02),
        "k_a": 1.0 + nrm(ks[10], (L, D_RWKV), 0.02),
        "r_k": nrm(ks[11], (L, D_RWKV), 0.1),
        "ln_x_g": 1.0 + nrm(ks[12], (L, D_RWKV), 0.02),
        "ln_x_b": nrm(ks[13], (L, D_RWKV), 0.01),
        "conv_w": nrm(ks[14], (L, CONV_WIDTH, D_CONV), CONV_WIDTH ** -0.5),
        "conv_b": nrm(ks[15], (L, D_CONV), 0.01),
        "conv_ln_g": 1.0 + nrm(ks[16], (L, D_CONV), 0.02),
        "conv_ln_b": nrm(ks[17], (L, D_CONV), 0.01),
        "w_out": nrm(ks[18], (L, D_MODEL, D_MODEL), D_MODEL ** -0.5),
        "norm_mlp_g": 1.0 + nrm(ks[19], (L, D_MODEL), 0.02),
        "w_ff1": nrm(ks[20], (L, D_MODEL, D_FF), D_MODEL ** -0.5),
        "w_ff2": nrm(ks[21], (L, D_FF, D_MODEL), D_FF ** -0.5),
        "norm_final_g": 1.0 + nrm(ks[22], (D_MODEL,), 0.02),
    }


def reference(x, norm_mix_g, w_in, shift_mu, w0, w_decay_up, a0, w_aaa_up, w_gate_up,
              k_k, k_a, r_k, ln_x_g, ln_x_b, conv_w, conv_b, conv_ln_g, conv_ln_b,
              w_out, norm_mlp_g, w_ff1, w_ff2, norm_final_g):
    for l in range(DEPTH):
        h = rmsnorm(x, norm_mix_g[l])
        p = h @ w_in[l]
        p_rwkv = p[..., :D_RWKV_IN]
        p_rwkv = p_rwkv + (token_shift(p_rwkv) - p_rwkv) * shift_mu[l]
        p_conv = p[..., D_RWKV_IN:]
        y_rwkv = rwkv7_group(p_rwkv, w0[l], w_decay_up[l], a0[l], w_aaa_up[l], w_gate_up[l],
                             k_k[l], k_a[l], r_k[l], ln_x_g[l], ln_x_b[l])
        y_conv = conformer_conv_group(p_conv, conv_w[l], conv_b[l], conv_ln_g[l], conv_ln_b[l])
        x = x + jnp.concatenate([y_rwkv, y_conv], axis=-1) @ w_out[l]
        h = rmsnorm(x, norm_mlp_g[l])
        x = x + jnp.square(jax.nn.relu(h @ w_ff1[l])) @ w_ff2[l]
    return rmsnorm(x, norm_final_g)
```

```python
import functools

import jax
import jax.numpy as jnp
from jax import lax
from jax.experimental import pallas as pl
from jax.experimental.pallas import tpu as pltpu

F32 = jnp.float32
BF16 = jnp.bfloat16

D_MODEL = 1024
D_RWKV = 512
HEAD = 64
N_HEADS = D_RWKV // HEAD
D_CONV = 512
CONV_WIDTH = 31
D_DECAY_LORA = 64
D_AAA_LORA = 64
D_GATE_LORA = 128
D_FF = 4 * D_MODEL
D_RWKV_IN = 3 * D_RWKV + D_DECAY_LORA + D_AAA_LORA + D_GATE_LORA
D_IN = D_RWKV_IN + 2 * D_CONV
RMS_EPS = 1e-6
GN_EPS = 64e-5
LN_EPS = 1e-5

LANES = 128
CHUNK = 64
CONV_HIST = 32
TM_IN = 256
TM_OUT = 512
FF_SPLIT = 4
VMEM_LIMIT = 56 * 1024 * 1024


def _sigmoid(z):
    return 1.0 / (1.0 + jnp.exp(-z))


def _dot(a, b):
    return jnp.dot(a, b, preferred_element_type=F32)


def _dot_nt(a, b):
    return lax.dot_general(a, b, (((1,), (1,)), ((), ())), preferred_element_type=F32)


def _mix_in_body(tiles_per_seq,
                 x_ref, g_ref, win_ref, mu_ref, w0_ref, wwa_ref, a0_ref, wg_ref,
                 kkp_ref, ka_ref, ones_ref, cw_ref, cb_ref, clg_ref, clb_ref,
                 r_out, lw_out, k_out, v_out, kk_out, b_out, g_out, yc_out,
                 carry_ref, hbuf_ref):
    tm = x_ref.shape[0]

    @pl.when(pl.program_id(0) % tiles_per_seq == 0)
    def _():
        carry_ref[...] = jnp.zeros_like(carry_ref)
        hbuf_ref[0:CONV_HIST, :] = jnp.zeros((CONV_HIST, D_CONV), F32)

    x = x_ref[...]
    ms = jnp.mean(x * x, axis=-1, keepdims=True)
    h = (x * lax.rsqrt(ms + RMS_EPS) * g_ref[...]).astype(BF16)
    p = _dot(h, win_ref[...])

    pr = p[:, :D_RWKV_IN]
    row = lax.broadcasted_iota(jnp.int32, pr.shape, 0)
    prev = jnp.where(row == 0, carry_ref[7:8, :], pltpu.roll(pr, 1, axis=0))
    carry_ref[...] = pr[tm - 8:tm, :]
    pm = pr + (prev - pr) * mu_ref[...]

    r = pm[:, 0:D_RWKV]
    k = pm[:, D_RWKV:2 * D_RWKV]
    v = pm[:, 2 * D_RWKV:3 * D_RWKV]
    lora_in = pm[:, 3 * D_RWKV:3 * D_RWKV + LANES]
    g_down = pm[:, 3 * D_RWKV + LANES:D_RWKV_IN]

    lane = lax.broadcasted_iota(jnp.int32, lora_in.shape, 1)
    lora_act = jnp.where(lane < D_DECAY_LORA, jnp.tanh(lora_in), lora_in).astype(BF16)
    wa = _dot(lora_act, wwa_ref[...])
    zw = -(w0_ref[...] + wa[:, :D_RWKV])
    softplus = jnp.maximum(zw, 0.0) + jnp.log(1.0 + jnp.exp(-jnp.abs(zw)))
    w_raw = -softplus - 0.5
    lw_out[...] = -jnp.exp(w_raw)
    a_sig = _sigmoid(a0_ref[...] + wa[:, D_RWKV:])
    g_out[...] = _dot(_sigmoid(g_down).astype(BF16), wg_ref[...])

    kk = k * kkp_ref[...]
    ss = _dot((kk * kk).astype(BF16), ones_ref[...])
    kk = kk / jnp.maximum(jnp.sqrt(ss), 1e-12)
    r_out[...] = r
    k_out[...] = k * (1.0 + (a_sig - 1.0) * ka_ref[...])
    v_out[...] = v
    kk_out[...] = kk
    b_out[...] = kk * a_sig

    u1 = p[:, D_RWKV_IN:D_RWKV_IN + D_CONV]
    u2 = p[:, D_RWKV_IN + D_CONV:D_IN]
    hbuf_ref[CONV_HIST:CONV_HIST + tm, :] = u1 * _sigmoid(u2)
    base = CONV_HIST - (CONV_WIDTH - 1)
    acc = jnp.broadcast_to(cb_ref[...], (tm, D_CONV))
    for j in range(CONV_WIDTH):
        acc = acc + cw_ref[j:j + 1, :] * hbuf_ref[pl.ds(base + j, tm), :]
    hbuf_ref[0:CONV_HIST, :] = hbuf_ref[tm:tm + CONV_HIST, :]
    mu = jnp.mean(acc, axis=-1, keepdims=True)
    d = acc - mu
    var = jnp.mean(d * d, axis=-1, keepdims=True)
    hn = d * lax.rsqrt(var + LN_EPS) * clg_ref[...] + clb_ref[...]
    yc_out[...] = (hn * _sigmoid(hn)).astype(BF16)


def _mix_in(x2, g, win, mu, w0, wwa, a0, wg, kkp, ka, ones, cw, cb, clg, clb, seq):
    n = x2.shape[0]
    tm = TM_IN
    full = lambda a: pl.BlockSpec(a.shape, lambda i: (0,) * a.ndim)
    tok = lambda w: pl.BlockSpec((tm, w), lambda i: (i, 0))
    consts = (g, win, mu, w0, wwa, a0, wg, kkp, ka, ones, cw, cb, clg, clb)
    out_f32 = jax.ShapeDtypeStruct((n, D_RWKV), F32)
    return pl.pallas_call(
        functools.partial(_mix_in_body, seq // tm),
        grid=(n // tm,),
        in_specs=[tok(D_MODEL)] + [full(a) for a in consts],
        out_specs=[tok(D_RWKV)] * 7 + [tok(D_CONV)],
        out_shape=[out_f32] * 7 + [jax.ShapeDtypeStruct((n, D_CONV), BF16)],
        scratch_shapes=[pltpu.VMEM((8, D_RWKV_IN), F32),
                        pltpu.VMEM((CONV_HIST + tm, D_CONV), F32)],
        compiler_params=pltpu.CompilerParams(
            dimension_semantics=("arbitrary",), vmem_limit_bytes=VMEM_LIMIT),
        name="mix_in",
    )(x2, *consts)


def _rwkv_rec_body(r_ref, lw_ref, k_ref, v_ref, kk_ref, b_ref, g_ref,
                   rk_ref, lng_ref, lnb_ref, ones_ref,
                   o_ref, st_ref, y_ref):
    L = CHUNK

    @pl.when(pl.program_id(1) == 0)
    def _():
        st_ref[...] = jnp.zeros_like(st_ref)

    lw = lw_ref[...]
    ri = lax.broadcasted_iota(jnp.int32, (L, L), 0)
    ci = lax.broadcasted_iota(jnp.int32, (L, L), 1)
    tri = jnp.where(ri >= ci, 1.0, 0.0).astype(BF16)
    lw_hi = lw.astype(BF16)
    rem = lw - lw_hi.astype(F32)
    lw_mid = rem.astype(BF16)
    lw_lo = (rem - lw_mid.astype(F32)).astype(BF16)
    lp = _dot(tri, lw_hi) + _dot(tri, lw_mid) + _dot(tri, lw_lo)
    last_row = lax.broadcasted_iota(jnp.int32, lp.shape, 0) == L - 1
    lp_end = jnp.sum(jnp.where(last_row, lp, 0.0), axis=0, keepdims=True)

    r = r_ref[...]
    k = k_ref[...]
    v = v_ref[...]
    kk = kk_ref[...]
    b = b_ref[...]
    e_neg = jnp.exp(-lp)
    e_end = jnp.exp(lp_end - lp)
    at = -(kk * jnp.exp(lp - lw))
    rt = r * jnp.exp(lp)
    bt = b * e_neg
    kt = k * e_neg
    bp = b * e_end
    kp = k * e_end
    p_end = jnp.exp(lp_end)

    lane = lax.broadcasted_iota(jnp.int32, (2 * L, LANES), 1)
    rowi = lax.broadcasted_iota(jnp.int32, (2 * L, LANES), 0)
    head_mask2 = (lane < HEAD, lane >= HEAD)
    t_idx = rowi & (L - 1)
    j_idx = lane & (L - 1)
    tri_mask = (j_idx < t_idx) | ((rowi >= L) & (j_idx == t_idx))
    lane1 = lax.broadcasted_iota(jnp.int32, (L, LANES), 1)
    row1 = lax.broadcasted_iota(jnp.int32, (L, LANES), 0)
    head_mask1 = (lane1 < HEAD, lane1 >= HEAD)
    diag_mask = row1 == lane1
    eye_pair = jnp.where(row1 == (lane1 & (L - 1)), 1.0, 0.0)
    zeros_b = jnp.zeros((L, LANES), BF16)

    for pair in range(N_HEADS // 2):
        sl = slice(pair * LANES, (pair + 1) * LANES)
        q_p = jnp.concatenate([at[:, sl], rt[:, sl]], axis=0)
        c_p = jnp.concatenate([bt[:, sl], kt[:, sl]], axis=0).astype(BF16)
        d_t = jnp.concatenate([bp[:, sl], kp[:, sl]], axis=0).T.astype(BF16)
        v_p = v[:, sl]
        q_roll = pltpu.roll(q_p, HEAD, axis=1)
        v_roll = pltpu.roll(v_p, HEAD, axis=1)
        pe_p = p_end[:, sl]
        pe_roll = pltpu.roll(pe_p, HEAD, axis=1)
        st_p = st_ref[pair]
        lhs_parts = []
        rhs_parts = []
        for hh in range(2):
            m1 = head_mask1[hh]
            a_all = _dot_nt(jnp.where(head_mask2[hh], q_p, 0.0).astype(BF16), c_p)
            a_all = jnp.where(tri_mask, a_all, 0.0).astype(BF16)
            a_top = a_all[:L]
            a_bot = a_all[L:]
            q_lo = jnp.where(head_mask2[0], q_p if hh == 0 else q_roll, 0.0)
            v_hi = jnp.where(head_mask1[1], v_p if hh == 1 else v_roll, 0.0).astype(BF16)
            zv = jnp.concatenate([zeros_b, v_hi], axis=0)
            xw = q_lo[:L] + _dot(a_top, zv)
            npow = a_top
            for it in range(6):
                xb = xw.astype(BF16)
                if it < 5:
                    rhs = jnp.concatenate(
                        [jnp.concatenate([xb, npow], axis=1),
                         jnp.zeros((L, 2 * LANES), BF16)], axis=0)
                    res = _dot(npow, rhs)
                    xw = xw + res[:, :LANES]
                    npow = res[:, LANES:].astype(BF16)
                else:
                    xw = xw + _dot(npow, jnp.concatenate([xb, zeros_b], axis=0))
            m2 = jnp.concatenate([xw.astype(BF16), v_hi], axis=0)
            ry = _dot(a_bot, m2) + q_lo[L:]
            gh = _dot(d_t[hh * HEAD:(hh + 1) * HEAD, :], m2)
            pl_row = pe_p if hh == 0 else pe_roll
            gh = gh + jnp.where(diag_mask, pl_row, 0.0)
            lhs_parts.append(jnp.concatenate([ry, gh], axis=0).astype(BF16))
            rhs_parts.append(jnp.concatenate(
                [jnp.where(m1, st_p, 0.0), jnp.where(m1, eye_pair, 0.0)], axis=0).astype(BF16))
        out = _dot(jnp.concatenate(lhs_parts, axis=1), jnp.concatenate(rhs_parts, axis=0))
        y_ref[:, sl] = out[:L]
        st_ref[pair] = out[L:]

    y = y_ref[...]
    ones = ones_ref[...]
    inv_n = 1.0 / HEAD
    mu = _dot(y.astype(BF16), ones) * inv_n
    d = y - mu
    var = _dot((d * d).astype(BF16), ones) * inv_n
    yn = d * lax.rsqrt(var + GN_EPS) * lng_ref[...] + lnb_ref[...]
    bonus = _dot((r * k * rk_ref[...]).astype(BF16), ones) * v
    o_ref[...] = ((yn + bonus) * g_ref[...]).astype(BF16)


def _rwkv_rec(r, lw, k, v, kk, b, g, rk, lng, lnb, ones, batch, seq):
    nchunk = seq // CHUNK
    tok = pl.BlockSpec((CHUNK, D_RWKV), lambda bi, ci: (bi * nchunk + ci, 0))
    full = lambda a: pl.BlockSpec(a.shape, lambda bi, ci: (0,) * a.ndim)
    consts = (rk, lng, lnb, ones)
    return pl.pallas_call(
        _rwkv_rec_body,
        grid=(batch, nchunk),
        in_specs=[tok] * 7 + [full(a) for a in consts],
        out_specs=tok,
        out_shape=jax.ShapeDtypeStruct((batch * seq, D_RWKV), BF16),
        scratch_shapes=[pltpu.VMEM((N_HEADS // 2, HEAD, LANES), F32),
                        pltpu.VMEM((CHUNK, D_RWKV), F32)],
        compiler_params=pltpu.CompilerParams(
            dimension_semantics=("arbitrary", "arbitrary"), vmem_limit_bytes=VMEM_LIMIT),
        name="rwkv_rec",
    )(r, lw, k, v, kk, b, g, *consts)


def _rms(x, g):
    ms = jnp.mean(x * x, axis=-1, keepdims=True)
    return x * lax.rsqrt(ms + RMS_EPS) * g


def _mix_out_body(x_ref, yr_ref, yc_ref, wo_ref, g2_ref, w1_ref, w2_ref, gf_ref, o_ref):
    y_cat = jnp.concatenate([yr_ref[...], yc_ref[...]], axis=1)
    x1 = x_ref[...] + _dot(y_cat, wo_ref[...])
    h = _rms(x1, g2_ref[...]).astype(BF16)
    acc = x1
    cols = D_FF // FF_SPLIT
    for c in range(FF_SPLIT):
        z = _dot(h, w1_ref[:, c * cols:(c + 1) * cols])
        z = jnp.square(jnp.maximum(z, 0.0)).astype(BF16)
        acc = acc + _dot(z, w2_ref[c * cols:(c + 1) * cols, :])
    o_ref[...] = _rms(acc, gf_ref[...])


def _mix_out(x2, yr, yc, wo, g2, w1, w2, gf):
    n = x2.shape[0]
    tm = TM_OUT
    tok = lambda w: pl.BlockSpec((tm, w), lambda i: (i, 0))
    full = lambda a: pl.BlockSpec(a.shape, lambda i: (0,) * a.ndim)
    consts = (wo, g2, w1, w2, gf)
    return pl.pallas_call(
        _mix_out_body,
        grid=(n // tm,),
        in_specs=[tok(D_MODEL), tok(D_RWKV), tok(D_CONV)] + [full(a) for a in consts],
        out_specs=tok(D_MODEL),
        out_shape=jax.ShapeDtypeStruct((n, D_MODEL), F32),
        compiler_params=pltpu.CompilerParams(
            dimension_semantics=("arbitrary",), vmem_limit_bytes=VMEM_LIMIT),
        name="mix_out",
    )(x2, yr, yc, *consts)


def kernel(x, norm_mix_g, w_in, shift_mu, w0, w_decay_up, a0, w_aaa_up, w_gate_up, k_k, k_a, r_k, ln_x_g, ln_x_b, conv_w, conv_b, conv_ln_g, conv_ln_b, w_out, norm_mlp_g, w_ff1, w_ff2, norm_final_g):
    batch, seq, _ = x.shape
    assert norm_mix_g.shape[0] == 1, "single layer"
    assert seq % TM_IN == 0 and seq % CHUNK == 0 and (batch * seq) % TM_OUT == 0
    row = lambda a: a.reshape(1, -1)
    x2 = x.reshape(batch * seq, D_MODEL)

    zer = jnp.zeros((D_DECAY_LORA, D_RWKV), F32)
    wwa = jnp.concatenate(
        [jnp.concatenate([w_decay_up[0], zer], axis=1),
         jnp.concatenate([zer, w_aaa_up[0]], axis=1)], axis=0).astype(BF16)
    hid = jnp.arange(D_RWKV) // HEAD
    ones = (hid[:, None] == hid[None, :]).astype(BF16)

    r, lw, k, v, kk, b, g, yc = _mix_in(
        x2, row(norm_mix_g[0]), w_in[0].astype(BF16), row(shift_mu[0]), row(w0[0]), wwa,
        row(a0[0]), w_gate_up[0].astype(BF16), row(k_k[0]), row(k_a[0]), ones,
        conv_w[0], row(conv_b[0]), row(conv_ln_g[0]), row(conv_ln_b[0]), seq)
    yr = _rwkv_rec(r, lw, k, v, kk, b, g, row(r_k[0]), row(ln_x_g[0]), row(ln_x_b[0]), ones,
                   batch, seq)
    out = _mix_out(x2, yr, yc, w_out[0].astype(BF16), row(norm_mlp_g[0]),
                   w_ff1[0].astype(BF16), w_ff2[0].astype(BF16), row(norm_final_g))
    return out.reshape(batch, seq, D_MODEL)
```

```python
import functools

import jax
import jax.numpy as jnp
from jax import lax
from jax.experimental import pallas as pl
from jax.experimental.pallas import tpu as pltpu

F32 = jnp.float32
BF16 = jnp.bfloat16

D_MODEL = 1024
D_RWKV = 512
HEAD = 64
N_HEADS = D_RWKV // HEAD
D_CONV = 512
CONV_WIDTH = 31
D_DECAY_LORA = 64
D_AAA_LORA = 64
D_GATE_LORA = 128
D_FF = 4 * D_MODEL
D_RWKV_IN = 3 * D_RWKV + D_DECAY_LORA + D_AAA_LORA + D_GATE_LORA
D_IN = D_RWKV_IN + 2 * D_CONV
RMS_EPS = 1e-6
GN_EPS = 64e-5
LN_EPS = 1e-5

LANES = 128
CHUNK = 64
CONV_HIST = 32
TM_IN = 256
TM_OUT = 512
FF_SPLIT = 4
VMEM_LIMIT = 56 * 1024 * 1024


def _sigmoid(z):
    return 1.0 / (1.0 + jnp.exp(-z))


def _dot(a, b):
    return jnp.dot(a, b, preferred_element_type=F32)


def _dot_nt(a, b):
    return lax.dot_general(a, b, (((1,), (1,)), ((), ())), preferred_element_type=F32)


def _mix_in_body(tiles_per_seq,
                 x_ref, g_ref, win_ref, mu_ref, w0_ref, wwa_ref, a0_ref, wg_ref,
                 kkp_ref, ka_ref, ones_ref, cw_ref, cb_ref, clg_ref, clb_ref,
                 r_out, lw_out, k_out, v_out, kk_out, b_out, g_out, yc_out,
                 carry_ref, hbuf_ref):
    tm = x_ref.shape[0]

    @pl.when(pl.program_id(0) % tiles_per_seq == 0)
    def _():
        carry_ref[...] = jnp.zeros_like(carry_ref)
        hbuf_ref[0:CONV_HIST, :] = jnp.zeros((CONV_HIST, D_CONV), F32)

    x = x_ref[...]
    ms = jnp.mean(x * x, axis=-1, keepdims=True)
    h = (x * lax.rsqrt(ms + RMS_EPS) * g_ref[...]).astype(BF16)
    p = _dot(h, win_ref[...])

    pr = p[:, :D_RWKV_IN]
    row = lax.broadcasted_iota(jnp.int32, pr.shape, 0)
    prev = jnp.where(row == 0, carry_ref[7:8, :], pltpu.roll(pr, 1, axis=0))
    carry_ref[...] = pr[tm - 8:tm, :]
    pm = pr + (prev - pr) * mu_ref[...]

    r = pm[:, 0:D_RWKV]
    k = pm[:, D_RWKV:2 * D_RWKV]
    v = pm[:, 2 * D_RWKV:3 * D_RWKV]
    lora_in = pm[:, 3 * D_RWKV:3 * D_RWKV + LANES]
    g_down = pm[:, 3 * D_RWKV + LANES:D_RWKV_IN]

    lane = lax.broadcasted_iota(jnp.int32, lora_in.shape, 1)
    lora_act = jnp.where(lane < D_DECAY_LORA, jnp.tanh(lora_in), lora_in).astype(BF16)
    wa = _dot(lora_act, wwa_ref[...])
    zw = -(w0_ref[...] + wa[:, :D_RWKV])
    softplus = jnp.maximum(zw, 0.0) + jnp.log(1.0 + jnp.exp(-jnp.abs(zw)))
    w_raw = -softplus - 0.5
    lw_out[...] = -jnp.exp(w_raw)
    a_sig = _sigmoid(a0_ref[...] + wa[:, D_RWKV:])
    g_out[...] = _dot(_sigmoid(g_down).astype(BF16), wg_ref[...])

    kk = k * kkp_ref[...]
    ss = _dot((kk * kk).astype(BF16), ones_ref[...])
    kk = kk / jnp.maximum(jnp.sqrt(ss), 1e-12)
    r_out[...] = r
    k_out[...] = k * (1.0 + (a_sig - 1.0) * ka_ref[...])
    v_out[...] = v
    kk_out[...] = kk
    b_out[...] = kk * a_sig

    u1 = p[:, D_RWKV_IN:D_RWKV_IN + D_CONV]
    u2 = p[:, D_RWKV_IN + D_CONV:D_IN]
    hbuf_ref[CONV_HIST:CONV_HIST + tm, :] = u1 * _sigmoid(u2)
    base = CONV_HIST - (CONV_WIDTH - 1)
    acc = jnp.broadcast_to(cb_ref[...], (tm, D_CONV))
    for j in range(CONV_WIDTH):
        acc = acc + cw_ref[j:j + 1, :] * hbuf_ref[pl.ds(base + j, tm), :]
    hbuf_ref[0:CONV_HIST, :] = hbuf_ref[tm:tm + CONV_HIST, :]
    mu = jnp.mean(acc, axis=-1, keepdims=True)
    d = acc - mu
    var = jnp.mean(d * d, axis=-1, keepdims=True)
    hn = d * lax.rsqrt(var + LN_EPS) * clg_ref[...] + clb_ref[...]
    yc_out[...] = (hn * _sigmoid(hn)).astype(BF16)


def _mix_in(x2, g, win, mu, w0, wwa, a0, wg, kkp, ka, ones, cw, cb, clg, clb, seq):
    n = x2.shape[0]
    tm = TM_IN
    full = lambda a: pl.BlockSpec(a.shape, lambda i: (0,) * a.ndim)
    tok = lambda w: pl.BlockSpec((tm, w), lambda i: (i, 0))
    consts = (g, win, mu, w0, wwa, a0, wg, kkp, ka, ones, cw, cb, clg, clb)
    out_f32 = jax.ShapeDtypeStruct((n, D_RWKV), F32)
    return pl.pallas_call(
        functools.partial(_mix_in_body, seq // tm),
        grid=(n // tm,),
        in_specs=[tok(D_MODEL)] + [full(a) for a in consts],
        out_specs=[tok(D_RWKV)] * 7 + [tok(D_CONV)],
        out_shape=[out_f32] * 7 + [jax.ShapeDtypeStruct((n, D_CONV), BF16)],
        scratch_shapes=[pltpu.VMEM((8, D_RWKV_IN), F32),
                        pltpu.VMEM((CONV_HIST + tm, D_CONV), F32)],
        compiler_params=pltpu.CompilerParams(
            dimension_semantics=("arbitrary",), vmem_limit_bytes=VMEM_LIMIT),
        name="mix_in",
    )(x2, *consts)


def _rwkv_rec_body(r_ref, lw_ref, k_ref, v_ref, kk_ref, b_ref, g_ref,
                   rk_ref, lng_ref, lnb_ref, ones_ref,
                   o_ref, st_ref, y_ref):
    L = CHUNK

    @pl.when(pl.program_id(1) == 0)
    def _():
        st_ref[...] = jnp.zeros_like(st_ref)

    lw = lw_ref[...]
    ri = lax.broadcasted_iota(jnp.int32, (L, L), 0)
    ci = lax.broadcasted_iota(jnp.int32, (L, L), 1)
    tri = jnp.where(ri >= ci, 1.0, 0.0).astype(BF16)
    lw_hi = lw.astype(BF16)
    rem = lw - lw_hi.astype(F32)
    lw_mid = rem.astype(BF16)
    lw_lo = (rem - lw_mid.astype(F32)).astype(BF16)
    lp = _dot(tri, lw_hi) + _dot(tri, lw_mid) + _dot(tri, lw_lo)
    last_row = lax.broadcasted_iota(jnp.int32, lp.shape, 0) == L - 1
    lp_end = jnp.sum(jnp.where(last_row, lp, 0.0), axis=0, keepdims=True)

    r = r_ref[...]
    k = k_ref[...]
    v = v_ref[...]
    kk = kk_ref[...]
    b = b_ref[...]
    e_neg = jnp.exp(-lp)
    e_end = jnp.exp(lp_end - lp)
    at = -(kk * jnp.exp(lp - lw))
    rt = r * jnp.exp(lp)
    bt = b * e_neg
    kt = k * e_neg
    bp = b * e_end
    kp = k * e_end
    p_end = jnp.exp(lp_end)

    lane = lax.broadcasted_iota(jnp.int32, (2 * L, LANES), 1)
    rowi = lax.broadcasted_iota(jnp.int32, (2 * L, LANES), 0)
    head_mask2 = (lane < HEAD, lane >= HEAD)
    t_idx = rowi & (L - 1)
    j_idx = lane & (L - 1)
    tri_mask = (j_idx < t_idx) | ((rowi >= L) & (j_idx == t_idx))
    lane1 = lax.broadcasted_iota(jnp.int32, (L, LANES), 1)
    row1 = lax.broadcasted_iota(jnp.int32, (L, LANES), 0)
    head_mask1 = (lane1 < HEAD, lane1 >= HEAD)
    diag_mask = row1 == lane1
    eye_pair = jnp.where(row1 == (lane1 & (L - 1)), 1.0, 0.0)
    zeros_b = jnp.zeros((L, LANES), BF16)

    pairs = range(N_HEADS // 2)
    heads = [(pr, hh) for pr in pairs for hh in range(2)]
    sls = [slice(pr * LANES, (pr + 1) * LANES) for pr in pairs]
    q_p = [jnp.concatenate([at[:, sl], rt[:, sl]], axis=0) for sl in sls]
    c_p = [jnp.concatenate([bt[:, sl], kt[:, sl]], axis=0).astype(BF16) for sl in sls]
    d_t = [jnp.concatenate([bp[:, sl], kp[:, sl]], axis=0).T.astype(BF16) for sl in sls]
    q_roll = [pltpu.roll(q, HEAD, axis=1) for q in q_p]
    v_roll = [pltpu.roll(v[:, sl], HEAD, axis=1) for sl in sls]
    pe_roll = [pltpu.roll(p_end[:, sl], HEAD, axis=1) for sl in sls]

    a_top, a_bot, q_lo, v_hi, xw, npow = [], [], [], [], [], []
    for pr, hh in heads:
        a_all = _dot_nt(jnp.where(head_mask2[hh], q_p[pr], 0.0).astype(BF16), c_p[pr])
        a_all = jnp.where(tri_mask, a_all, 0.0).astype(BF16)
        a_top.append(a_all[:L])
        a_bot.append(a_all[L:])
        q_lo.append(jnp.where(head_mask2[0], q_p[pr] if hh == 0 else q_roll[pr], 0.0))
        v_hi.append(jnp.where(head_mask1[1], v[:, sls[pr]] if hh == 1 else v_roll[pr], 0.0).astype(BF16))
    for i in range(N_HEADS):
        zv = jnp.concatenate([zeros_b, v_hi[i]], axis=0)
        xw.append(q_lo[i][:L] + _dot(a_top[i], zv))
        npow.append(a_top[i])
    for it in range(6):
        for i in range(N_HEADS):
            xb = xw[i].astype(BF16)
            if it < 5:
                rhs = jnp.concatenate(
                    [jnp.concatenate([xb, npow[i]], axis=1),
                     jnp.zeros((L, 2 * LANES), BF16)], axis=0)
                res = _dot(npow[i], rhs)
                xw[i] = xw[i] + res[:, :LANES]
                npow[i] = res[:, LANES:].astype(BF16)
            else:
                xw[i] = xw[i] + _dot(npow[i], jnp.concatenate([xb, zeros_b], axis=0))
    lhs2, rhs2 = [], []
    for i, (pr, hh) in enumerate(heads):
        m2 = jnp.concatenate([xw[i].astype(BF16), v_hi[i]], axis=0)
        ry = _dot(a_bot[i], m2) + q_lo[i][L:]
        gh = _dot(d_t[pr][hh * HEAD:(hh + 1) * HEAD, :], m2)
        pl_row = p_end[:, sls[pr]] if hh == 0 else pe_roll[pr]
        gh = gh + jnp.where(diag_mask, pl_row, 0.0)
        lhs2.append(jnp.concatenate([ry, gh], axis=0).astype(BF16))
        m1 = head_mask1[hh]
        st_p = st_ref[pr]
        rhs2.append(jnp.concatenate(
            [jnp.where(m1, st_p, 0.0), jnp.where(m1, eye_pair, 0.0)], axis=0).astype(BF16))
    for pr in pairs:
        out = _dot(jnp.concatenate(lhs2[2 * pr:2 * pr + 2], axis=1),
                   jnp.concatenate(rhs2[2 * pr:2 * pr + 2], axis=0))
        y_ref[:, sls[pr]] = out[:L]
        st_ref[pr] = out[L:]

    y = y_ref[...]
    ones = ones_ref[...]
    inv_n = 1.0 / HEAD
    mu = _dot(y.astype(BF16), ones) * inv_n
    d = y - mu
    var = _dot((d * d).astype(BF16), ones) * inv_n
    yn = d * lax.rsqrt(var + GN_EPS) * lng_ref[...] + lnb_ref[...]
    bonus = _dot((r * k * rk_ref[...]).astype(BF16), ones) * v
    o_ref[...] = ((yn + bonus) * g_ref[...]).astype(BF16)


def _rwkv_rec(r, lw, k, v, kk, b, g, rk, lng, lnb, ones, batch, seq):
    nchunk = seq // CHUNK
    tok = pl.BlockSpec((CHUNK, D_RWKV), lambda bi, ci: (bi * nchunk + ci, 0))
    full = lambda a: pl.BlockSpec(a.shape, lambda bi, ci: (0,) * a.ndim)
    consts = (rk, lng, lnb, ones)
    return pl.pallas_call(
        _rwkv_rec_body,
        grid=(batch, nchunk),
        in_specs=[tok] * 7 + [full(a) for a in consts],
        out_specs=tok,
        out_shape=jax.ShapeDtypeStruct((batch * seq, D_RWKV), BF16),
        scratch_shapes=[pltpu.VMEM((N_HEADS // 2, HEAD, LANES), F32),
                        pltpu.VMEM((CHUNK, D_RWKV), F32)],
        compiler_params=pltpu.CompilerParams(
            dimension_semantics=("arbitrary", "arbitrary"), vmem_limit_bytes=VMEM_LIMIT),
        name="rwkv_rec",
    )(r, lw, k, v, kk, b, g, *consts)


def _rms(x, g):
    ms = jnp.mean(x * x, axis=-1, keepdims=True)
    return x * lax.rsqrt(ms + RMS_EPS) * g


def _mix_out_body(x_ref, yr_ref, yc_ref, wo_ref, g2_ref, w1_ref, w2_ref, gf_ref, o_ref):
    y_cat = jnp.concatenate([yr_ref[...], yc_ref[...]], axis=1)
    x1 = x_ref[...] + _dot(y_cat, wo_ref[...])
    h = _rms(x1, g2_ref[...]).astype(BF16)
    acc = x1
    cols = D_FF // FF_SPLIT
    for c in range(FF_SPLIT):
        z = _dot(h, w1_ref[:, c * cols:(c + 1) * cols])
        z = jnp.square(jnp.maximum(z, 0.0)).astype(BF16)
        acc = acc + _dot(z, w2_ref[c * cols:(c + 1) * cols, :])
    o_ref[...] = _rms(acc, gf_ref[...])


def _mix_out(x2, yr, yc, wo, g2, w1, w2, gf):
    n = x2.shape[0]
    tm = TM_OUT
    tok = lambda w: pl.BlockSpec((tm, w), lambda i: (i, 0))
    full = lambda a: pl.BlockSpec(a.shape, lambda i: (0,) * a.ndim)
    consts = (wo, g2, w1, w2, gf)
    return pl.pallas_call(
        _mix_out_body,
        grid=(n // tm,),
        in_specs=[tok(D_MODEL), tok(D_RWKV), tok(D_CONV)] + [full(a) for a in consts],
        out_specs=tok(D_MODEL),
        out_shape=jax.ShapeDtypeStruct((n, D_MODEL), F32),
        compiler_params=pltpu.CompilerParams(
            dimension_semantics=("arbitrary",), vmem_limit_bytes=VMEM_LIMIT),
        name="mix_out",
    )(x2, yr, yc, *consts)


def kernel(x, norm_mix_g, w_in, shift_mu, w0, w_decay_up, a0, w_aaa_up, w_gate_up, k_k, k_a, r_k, ln_x_g, ln_x_b, conv_w, conv_b, conv_ln_g, conv_ln_b, w_out, norm_mlp_g, w_ff1, w_ff2, norm_final_g):
    batch, seq, _ = x.shape
    assert norm_mix_g.shape[0] == 1, "single layer"
    assert seq % TM_IN == 0 and seq % CHUNK == 0 and (batch * seq) % TM_OUT == 0
    row = lambda a: a.reshape(1, -1)
    x2 = x.reshape(batch * seq, D_MODEL)

    zer = jnp.zeros((D_DECAY_LORA, D_RWKV), F32)
    wwa = jnp.concatenate(
        [jnp.concatenate([w_decay_up[0], zer], axis=1),
         jnp.concatenate([zer, w_aaa_up[0]], axis=1)], axis=0).astype(BF16)
    hid = jnp.arange(D_RWKV) // HEAD
    ones = (hid[:, None] == hid[None, :]).astype(BF16)

    r, lw, k, v, kk, b, g, yc = _mix_in(
        x2, row(norm_mix_g[0]), w_in[0].astype(BF16), row(shift_mu[0]), row(w0[0]), wwa,
        row(a0[0]), w_gate_up[0].astype(BF16), row(k_k[0]), row(k_a[0]), ones,
        conv_w[0], row(conv_b[0]), row(conv_ln_g[0]), row(conv_ln_b[0]), seq)
    yr = _rwkv_rec(r, lw, k, v, kk, b, g, row(r_k[0]), row(ln_x_g[0]), row(ln_x_b[0]), ones,
                   batch, seq)
    out = _mix_out(x2, yr, yc, w_out[0].astype(BF16), row(norm_mlp_g[0]),
                   w_ff1[0].astype(BF16), w_ff2[0].astype(BF16), row(norm_final_g))
    return out.reshape(batch, seq, D_MODEL)
```

```python
import functools

import jax
import jax.numpy as jnp
from jax import lax
from jax.experimental import pallas as pl
from jax.experimental.pallas import tpu as pltpu

F32 = jnp.float32
BF16 = jnp.bfloat16

D_MODEL = 1024
D_RWKV = 512
HEAD = 64
N_HEADS = D_RWKV // HEAD
D_CONV = 512
CONV_WIDTH = 31
D_DECAY_LORA = 64
D_AAA_LORA = 64
D_GATE_LORA = 128
D_FF = 4 * D_MODEL
D_RWKV_IN = 3 * D_RWKV + D_DECAY_LORA + D_AAA_LORA + D_GATE_LORA
D_IN = D_RWKV_IN + 2 * D_CONV
RMS_EPS = 1e-6
GN_EPS = 64e-5
LN_EPS = 1e-5

LANES = 128
SUBLANES = 8
CHUNK = 64
REC_CHUNKS = 4
CONV_HIST = 32
TM_IN = 256
TM_OUT = 512
FF_SPLIT = 4
VMEM_LIMIT = 56 * 1024 * 1024


def _sigmoid(z):
    return 1.0 / (1.0 + jnp.exp(-z))


def _dot(a, b):
    return jnp.dot(a, b, preferred_element_type=F32)


def _dot_nt(a, b):
    return lax.dot_general(a, b, (((1,), (1,)), ((), ())), preferred_element_type=F32)


def _mix_in_body(tiles_per_seq,
                 x_ref, g_ref, win_ref, mu_ref, w0_ref, wwa_ref, a0_ref, wg_ref,
                 kkp_ref, ka_ref, ones_ref, cw_ref, cb_ref, clg_ref, clb_ref,
                 r_out, lw_out, k_out, v_out, kk_out, b_out, g_out, yc_out,
                 carry_ref, hbuf_ref):
    tm = x_ref.shape[0]

    @pl.when(pl.program_id(0) % tiles_per_seq == 0)
    def _():
        carry_ref[...] = jnp.zeros_like(carry_ref)
        hbuf_ref[0, 0:CONV_HIST, :] = jnp.zeros((CONV_HIST, D_CONV), F32)

    x = x_ref[...]
    ms = jnp.mean(x * x, axis=-1, keepdims=True)
    h = (x * lax.rsqrt(ms + RMS_EPS) * g_ref[...]).astype(BF16)
    p = _dot(h, win_ref[...])

    pr = p[:, :D_RWKV_IN]
    row = lax.broadcasted_iota(jnp.int32, pr.shape, 0)
    prev = jnp.where(row == 0, carry_ref[7:8, :], pltpu.roll(pr, 1, axis=0))
    carry_ref[...] = pr[tm - 8:tm, :]
    pm = pr + (prev - pr) * mu_ref[...]

    r = pm[:, 0:D_RWKV]
    k = pm[:, D_RWKV:2 * D_RWKV]
    v = pm[:, 2 * D_RWKV:3 * D_RWKV]
    lora_in = pm[:, 3 * D_RWKV:3 * D_RWKV + LANES]
    g_down = pm[:, 3 * D_RWKV + LANES:D_RWKV_IN]

    lane = lax.broadcasted_iota(jnp.int32, lora_in.shape, 1)
    lora_act = jnp.where(lane < D_DECAY_LORA, jnp.tanh(lora_in), lora_in).astype(BF16)
    wa = _dot(lora_act, wwa_ref[...])
    zw = -(w0_ref[...] + wa[:, :D_RWKV])
    softplus = jnp.maximum(zw, 0.0) + jnp.log(1.0 + jnp.exp(-jnp.abs(zw)))
    w_raw = -softplus - 0.5
    lw_out[...] = -jnp.exp(w_raw)
    a_sig = _sigmoid(a0_ref[...] + wa[:, D_RWKV:])
    g_out[...] = _dot(_sigmoid(g_down).astype(BF16), wg_ref[...])

    kk = k * kkp_ref[...]
    ss = _dot((kk * kk).astype(BF16), ones_ref[...])
    kk = kk / jnp.maximum(jnp.sqrt(ss), 1e-12)
    r_out[...] = r
    k_out[...] = k * (1.0 + (a_sig - 1.0) * ka_ref[...])
    v_out[...] = v
    kk_out[...] = kk
    b_out[...] = kk * a_sig

    u1 = p[:, D_RWKV_IN:D_RWKV_IN + D_CONV]
    u2 = p[:, D_RWKV_IN + D_CONV:D_IN]
    hbuf_ref[0, CONV_HIST:CONV_HIST + tm, :] = u1 * _sigmoid(u2)
    span = tm + CONV_HIST - SUBLANES
    for s in range(1, SUBLANES):
        hbuf_ref[s, 0:span, :] = hbuf_ref[0, pl.ds(s, span), :]
    base = CONV_HIST - (CONV_WIDTH - 1)
    acc = jnp.broadcast_to(cb_ref[...], (tm, D_CONV))
    for j in range(CONV_WIDTH):
        off = base + j
        acc = acc + cw_ref[j:j + 1, :] * hbuf_ref[off % SUBLANES, pl.ds(off - off % SUBLANES, tm), :]
    hbuf_ref[0, 0:CONV_HIST, :] = hbuf_ref[0, tm:tm + CONV_HIST, :]
    mu = jnp.mean(acc, axis=-1, keepdims=True)
    d = acc - mu
    var = jnp.mean(d * d, axis=-1, keepdims=True)
    hn = d * lax.rsqrt(var + LN_EPS) * clg_ref[...] + clb_ref[...]
    yc_out[...] = (hn * _sigmoid(hn)).astype(BF16)


def _mix_in(x2, g, win, mu, w0, wwa, a0, wg, kkp, ka, ones, cw, cb, clg, clb, seq):
    n = x2.shape[0]
    tm = TM_IN
    full = lambda a: pl.BlockSpec(a.shape, lambda i: (0,) * a.ndim)
    tok = lambda w: pl.BlockSpec((tm, w), lambda i: (i, 0))
    consts = (g, win, mu, w0, wwa, a0, wg, kkp, ka, ones, cw, cb, clg, clb)
    out_f32 = jax.ShapeDtypeStruct((n, D_RWKV), F32)
    return pl.pallas_call(
        functools.partial(_mix_in_body, seq // tm),
        grid=(n // tm,),
        in_specs=[tok(D_MODEL)] + [full(a) for a in consts],
        out_specs=[tok(D_RWKV)] * 7 + [tok(D_CONV)],
        out_shape=[out_f32] * 7 + [jax.ShapeDtypeStruct((n, D_CONV), BF16)],
        scratch_shapes=[pltpu.VMEM((8, D_RWKV_IN), F32),
                        pltpu.VMEM((SUBLANES, CONV_HIST + tm, D_CONV), F32)],
        compiler_params=pltpu.CompilerParams(
            dimension_semantics=("arbitrary",), vmem_limit_bytes=VMEM_LIMIT),
        name="mix_in",
    )(x2, *consts)


def _rwkv_rec_body(r_ref, lw_ref, k_ref, v_ref, kk_ref, b_ref, g_ref,
                   rk_ref, lng_ref, lnb_ref, ones_ref,
                   o_ref, st_ref, y_ref):
    L = CHUNK
    nch = r_ref.shape[0] // L

    @pl.when(pl.program_id(1) == 0)
    def _():
        st_ref[...] = jnp.zeros_like(st_ref)

    ri = lax.broadcasted_iota(jnp.int32, (L, L), 0)
    ci = lax.broadcasted_iota(jnp.int32, (L, L), 1)
    tri = jnp.where(ri >= ci, 1.0, 0.0).astype(BF16)
    last_row = lax.broadcasted_iota(jnp.int32, (L, D_RWKV), 0) == L - 1
    lane = lax.broadcasted_iota(jnp.int32, (2 * L, LANES), 1)
    rowi = lax.broadcasted_iota(jnp.int32, (2 * L, LANES), 0)
    head_mask2 = (lane < HEAD, lane >= HEAD)
    t_idx = rowi & (L - 1)
    j_idx = lane & (L - 1)
    tri_mask = (j_idx < t_idx) | ((rowi >= L) & (j_idx == t_idx))
    lane1 = lax.broadcasted_iota(jnp.int32, (L, LANES), 1)
    row1 = lax.broadcasted_iota(jnp.int32, (L, LANES), 0)
    head_mask1 = (lane1 < HEAD, lane1 >= HEAD)
    diag_mask = row1 == lane1
    eye_pair = jnp.where(row1 == (lane1 & (L - 1)), 1.0, 0.0)
    zeros_b = jnp.zeros((L, LANES), BF16)
    pairs = range(N_HEADS // 2)
    sls = [slice(pr * LANES, (pr + 1) * LANES) for pr in pairs]

    q_p, c_p, d_t, q_roll, v_p, v_roll, pe_p, pe_roll = ({} for _ in range(8))
    for cc in range(nch):
        rows = pl.ds(cc * L, L)
        lw = lw_ref[rows, :]
        lw_hi = lw.astype(BF16)
        rem = lw - lw_hi.astype(F32)
        lw_mid = rem.astype(BF16)
        lw_lo = (rem - lw_mid.astype(F32)).astype(BF16)
        lp = _dot(tri, lw_hi) + _dot(tri, lw_mid) + _dot(tri, lw_lo)
        lp_end = jnp.sum(jnp.where(last_row, lp, 0.0), axis=0, keepdims=True)
        k = k_ref[rows, :]
        v = v_ref[rows, :]
        b = b_ref[rows, :]
        e_neg = jnp.exp(-lp)
        e_end = jnp.exp(lp_end - lp)
        at = -(kk_ref[rows, :] * jnp.exp(lp - lw))
        rt = r_ref[rows, :] * jnp.exp(lp)
        bt = b * e_neg
        kt = k * e_neg
        bp = b * e_end
        kp = k * e_end
        p_end = jnp.exp(lp_end)
        for pr in pairs:
            sl = sls[pr]
            u = (cc, pr)
            q_p[u] = jnp.concatenate([at[:, sl], rt[:, sl]], axis=0)
            c_p[u] = jnp.concatenate([bt[:, sl], kt[:, sl]], axis=0).astype(BF16)
            d_t[u] = jnp.concatenate([bp[:, sl], kp[:, sl]], axis=0).T.astype(BF16)
            q_roll[u] = pltpu.roll(q_p[u], HEAD, axis=1)
            v_p[u] = v[:, sl]
            v_roll[u] = pltpu.roll(v_p[u], HEAD, axis=1)
            pe_p[u] = p_end[:, sl]
            pe_roll[u] = pltpu.roll(pe_p[u], HEAD, axis=1)

    units = [(cc, pr, hh) for cc in range(nch) for pr in pairs for hh in range(2)]
    a_top, a_bot, q_lo, v_hi, xw, npow = [], [], [], [], [], []
    for cc, pr, hh in units:
        u = (cc, pr)
        a_all = _dot_nt(jnp.where(head_mask2[hh], q_p[u], 0.0).astype(BF16), c_p[u])
        a_all = jnp.where(tri_mask, a_all, 0.0).astype(BF16)
        a_top.append(a_all[:L])
        a_bot.append(a_all[L:])
        q_lo.append(jnp.where(head_mask2[0], q_p[u] if hh == 0 else q_roll[u], 0.0))
        v_hi.append(jnp.where(head_mask1[1], v_p[u] if hh == 1 else v_roll[u], 0.0).astype(BF16))
    for i in range(len(units)):
        zv = jnp.concatenate([zeros_b, v_hi[i]], axis=0)
        xw.append(q_lo[i][:L] + _dot(a_top[i], zv))
        npow.append(a_top[i])
    for it in range(6):
        for i in range(len(units)):
            xb = xw[i].astype(BF16)
            if it < 5:
                rhs = jnp.concatenate(
                    [jnp.concatenate([xb, npow[i]], axis=1),
                     jnp.zeros((L, 2 * LANES), BF16)], axis=0)
                res = _dot(npow[i], rhs)
                xw[i] = xw[i] + res[:, :LANES]
                npow[i] = res[:, LANES:].astype(BF16)
            else:
                xw[i] = xw[i] + _dot(npow[i], jnp.concatenate([xb, zeros_b], axis=0))
    lhs2 = {}
    for i, (cc, pr, hh) in enumerate(units):
        u = (cc, pr)
        m2 = jnp.concatenate([xw[i].astype(BF16), v_hi[i]], axis=0)
        ry = _dot(a_bot[i], m2) + q_lo[i][L:]
        gh = _dot(d_t[u][hh * HEAD:(hh + 1) * HEAD, :], m2)
        gh = gh + jnp.where(diag_mask, pe_p[u] if hh == 0 else pe_roll[u], 0.0)
        lhs2[(cc, pr, hh)] = jnp.concatenate([ry, gh], axis=0).astype(BF16)

    eye_rows = [jnp.where(head_mask1[hh], eye_pair, 0.0).astype(BF16) for hh in range(2)]
    st = [st_ref[pr] for pr in pairs]
    for cc in range(nch):
        for pr in pairs:
            rhs2 = jnp.concatenate(
                [jnp.where(head_mask1[0], st[pr], 0.0).astype(BF16), eye_rows[0],
                 jnp.where(head_mask1[1], st[pr], 0.0).astype(BF16), eye_rows[1]], axis=0)
            out = _dot(jnp.concatenate([lhs2[(cc, pr, 0)], lhs2[(cc, pr, 1)]], axis=1), rhs2)
            y_ref[pl.ds(cc * L, L), sls[pr]] = out[:L]
            st[pr] = out[L:]
    for pr in pairs:
        st_ref[pr] = st[pr]

    y = y_ref[...]
    ones = ones_ref[...]
    inv_n = 1.0 / HEAD
    mu = _dot(y.astype(BF16), ones) * inv_n
    d = y - mu
    var = _dot((d * d).astype(BF16), ones) * inv_n
    yn = d * lax.rsqrt(var + GN_EPS) * lng_ref[...] + lnb_ref[...]
    bonus = _dot((r_ref[...] * k_ref[...] * rk_ref[...]).astype(BF16), ones) * v_ref[...]
    o_ref[...] = ((yn + bonus) * g_ref[...]).astype(BF16)


def _rwkv_rec(r, lw, k, v, kk, b, g, rk, lng, lnb, ones, batch, seq):
    rows = REC_CHUNKS * CHUNK
    nstep = seq // rows
    tok = pl.BlockSpec((rows, D_RWKV), lambda bi, ci: (bi * nstep + ci, 0))
    full = lambda a: pl.BlockSpec(a.shape, lambda bi, ci: (0,) * a.ndim)
    consts = (rk, lng, lnb, ones)
    return pl.pallas_call(
        _rwkv_rec_body,
        grid=(batch, nstep),
        in_specs=[tok] * 7 + [full(a) for a in consts],
        out_specs=tok,
        out_shape=jax.ShapeDtypeStruct((batch * seq, D_RWKV), BF16),
        scratch_shapes=[pltpu.VMEM((N_HEADS // 2, HEAD, LANES), F32),
                        pltpu.VMEM((rows, D_RWKV), F32)],
        compiler_params=pltpu.CompilerParams(
            dimension_semantics=("arbitrary", "arbitrary"), vmem_limit_bytes=VMEM_LIMIT),
        name="rwkv_rec",
    )(r, lw, k, v, kk, b, g, *consts)


def _rms(x, g):
    ms = jnp.mean(x * x, axis=-1, keepdims=True)
    return x * lax.rsqrt(ms + RMS_EPS) * g


def _mix_out_body(x_ref, yr_ref, yc_ref, wo_ref, g2_ref, w1_ref, w2_ref, gf_ref, o_ref):
    y_cat = jnp.concatenate([yr_ref[...], yc_ref[...]], axis=1)
    x1 = x_ref[...] + _dot(y_cat, wo_ref[...])
    h = _rms(x1, g2_ref[...]).astype(BF16)
    acc = x1
    cols = D_FF // FF_SPLIT
    for c in range(FF_SPLIT):
        z = _dot(h, w1_ref[:, c * cols:(c + 1) * cols])
        z = jnp.square(jnp.maximum(z, 0.0)).astype(BF16)
        acc = acc + _dot(z, w2_ref[c * cols:(c + 1) * cols, :])
    o_ref[...] = _rms(acc, gf_ref[...])


def _mix_out(x2, yr, yc, wo, g2, w1, w2, gf):
    n = x2.shape[0]
    tm = TM_OUT
    tok = lambda w: pl.BlockSpec((tm, w), lambda i: (i, 0))
    full = lambda a: pl.BlockSpec(a.shape, lambda i: (0,) * a.ndim)
    consts = (wo, g2, w1, w2, gf)
    return pl.pallas_call(
        _mix_out_body,
        grid=(n // tm,),
        in_specs=[tok(D_MODEL), tok(D_RWKV), tok(D_CONV)] + [full(a) for a in consts],
        out_specs=tok(D_MODEL),
        out_shape=jax.ShapeDtypeStruct((n, D_MODEL), F32),
        compiler_params=pltpu.CompilerParams(
            dimension_semantics=("arbitrary",), vmem_limit_bytes=VMEM_LIMIT),
        name="mix_out",
    )(x2, yr, yc, *consts)


def kernel(x, norm_mix_g, w_in, shift_mu, w0, w_decay_up, a0, w_aaa_up, w_gate_up, k_k, k_a, r_k, ln_x_g, ln_x_b, conv_w, conv_b, conv_ln_g, conv_ln_b, w_out, norm_mlp_g, w_ff1, w_ff2, norm_final_g):
    batch, seq, _ = x.shape
    assert norm_mix_g.shape[0] == 1, "single layer"
    assert seq % TM_IN == 0 and seq % (REC_CHUNKS * CHUNK) == 0 and (batch * seq) % TM_OUT == 0
    row = lambda a: a.reshape(1, -1)
    x2 = x.reshape(batch * seq, D_MODEL)

    zer = jnp.zeros((D_DECAY_LORA, D_RWKV), F32)
    wwa = jnp.concatenate(
        [jnp.concatenate([w_decay_up[0], zer], axis=1),
         jnp.concatenate([zer, w_aaa_up[0]], axis=1)], axis=0).astype(BF16)
    hid = jnp.arange(D_RWKV) // HEAD
    ones = (hid[:, None] == hid[None, :]).astype(BF16)

    r, lw, k, v, kk, b, g, yc = _mix_in(
        x2, row(norm_mix_g[0]), w_in[0].astype(BF16), row(shift_mu[0]), row(w0[0]), wwa,
        row(a0[0]), w_gate_up[0].astype(BF16), row(k_k[0]), row(k_a[0]), ones,
        conv_w[0], row(conv_b[0]), row(conv_ln_g[0]), row(conv_ln_b[0]), seq)
    yr = _rwkv_rec(r, lw, k, v, kk, b, g, row(r_k[0]), row(ln_x_g[0]), row(ln_x_b[0]), ones,
                   batch, seq)
    out = _mix_out(x2, yr, yc, w_out[0].astype(BF16), row(norm_mlp_g[0]),
                   w_ff1[0].astype(BF16), w_ff2[0].astype(BF16), row(norm_final_g))
    return out.reshape(batch, seq, D_MODEL)
```

```python
import functools

import jax
import jax.numpy as jnp
from jax import lax
from jax.experimental import pallas as pl
from jax.experimental.pallas import tpu as pltpu

F32 = jnp.float32
BF16 = jnp.bfloat16

D_MODEL = 1024
D_RWKV = 512
HEAD = 64
N_HEADS = D_RWKV // HEAD
D_CONV = 512
CONV_WIDTH = 31
D_DECAY_LORA = 64
D_AAA_LORA = 64
D_GATE_LORA = 128
D_FF = 4 * D_MODEL
D_RWKV_IN = 3 * D_RWKV + D_DECAY_LORA + D_AAA_LORA + D_GATE_LORA
D_IN = D_RWKV_IN + 2 * D_CONV
RMS_EPS = 1e-6
GN_EPS = 64e-5
LN_EPS = 1e-5
KK_NORM_FLOOR = 1e-12
DECAY_SCALE = 0.6065306597126334

LANES = 128
SUBLANES = 8
CHUNK = 64
REC_CHUNKS = 8
CONV_HIST = 32
CONV_ROWS = 32
TM_IN = 256
TM_OUT = 512
FF_SPLIT = 4
VMEM_LIMIT = 56 * 1024 * 1024


def _sigmoid(z):
    return 1.0 / (1.0 + jnp.exp(-z))


def _dot(a, b):
    return jnp.dot(a, b, preferred_element_type=F32)


def _dot_nt(a, b):
    return lax.dot_general(a, b, (((1,), (1,)), ((), ())), preferred_element_type=F32)


def _mix_in_body(tiles_per_seq,
                 x_ref, g_ref, win_ref, mu_ref, w0_ref, wwa_ref, a0_ref, wg_ref,
                 kkp_ref, ka_ref, ones_ref, cw_ref, cb_ref, clg_ref, clb_ref,
                 r_out, lw_out, k_out, v_out, kk_out, b_out, g_out, yc_out,
                 carry_ref, hbuf_ref):
    tm = x_ref.shape[0]

    @pl.when(pl.program_id(0) % tiles_per_seq == 0)
    def _():
        carry_ref[...] = jnp.zeros_like(carry_ref)
        hbuf_ref[0, 0:CONV_HIST, :] = jnp.zeros((CONV_HIST, D_CONV), F32)

    x = x_ref[...]
    ms = jnp.mean(x * x, axis=-1, keepdims=True)
    h = (x * lax.rsqrt(ms + RMS_EPS) * g_ref[...]).astype(BF16)
    p = _dot(h, win_ref[...])

    pr = p[:, :D_RWKV_IN]
    row = lax.broadcasted_iota(jnp.int32, pr.shape, 0)
    prev = jnp.where(row == 0, carry_ref[7:8, :], pltpu.roll(pr, 1, axis=0))
    carry_ref[...] = pr[tm - 8:tm, :]
    pm = pr + (prev - pr) * mu_ref[...]

    r = pm[:, 0:D_RWKV]
    k = pm[:, D_RWKV:2 * D_RWKV]
    v = pm[:, 2 * D_RWKV:3 * D_RWKV]
    lora_in = pm[:, 3 * D_RWKV:3 * D_RWKV + LANES]
    g_down = pm[:, 3 * D_RWKV + LANES:D_RWKV_IN]

    lane = lax.broadcasted_iota(jnp.int32, lora_in.shape, 1)
    lora_act = jnp.where(lane < D_DECAY_LORA, jnp.tanh(lora_in), lora_in).astype(BF16)
    wa = _dot(lora_act, wwa_ref[...])
    lw_out[...] = -DECAY_SCALE * _sigmoid(w0_ref[...] + wa[:, :D_RWKV])
    a_sig = _sigmoid(a0_ref[...] + wa[:, D_RWKV:])
    g_out[...] = _dot(_sigmoid(g_down).astype(BF16), wg_ref[...])

    kk = k * kkp_ref[...]
    ss = _dot((kk * kk).astype(BF16), ones_ref[...])
    kk = kk * lax.rsqrt(jnp.maximum(ss, KK_NORM_FLOOR * KK_NORM_FLOOR))
    r_out[...] = r
    k_out[...] = k * (1.0 + (a_sig - 1.0) * ka_ref[...])
    v_out[...] = v
    kk_out[...] = kk
    b_out[...] = kk * a_sig

    u1 = p[:, D_RWKV_IN:D_RWKV_IN + D_CONV]
    u2 = p[:, D_RWKV_IN + D_CONV:D_IN]
    hbuf_ref[0, CONV_HIST:CONV_HIST + tm, :] = u1 * _sigmoid(u2)
    span = tm + CONV_HIST - SUBLANES
    for s in range(1, SUBLANES):
        hbuf_ref[s, 0:span, :] = hbuf_ref[0, pl.ds(s, span), :]
    base = CONV_HIST - (CONV_WIDTH - 1)
    for r0 in range(0, tm, CONV_ROWS):
        acc = jnp.broadcast_to(cb_ref[...], (CONV_ROWS, D_CONV))
        for j in range(CONV_WIDTH):
            off = base + j
            wj = jnp.concatenate([cw_ref[j]] * (CONV_ROWS // SUBLANES), axis=0)
            acc = acc + wj * hbuf_ref[off % SUBLANES, pl.ds(r0 + off - off % SUBLANES, CONV_ROWS), :]
        mu = jnp.mean(acc, axis=-1, keepdims=True)
        d = acc - mu
        var = jnp.mean(d * d, axis=-1, keepdims=True)
        hn = d * lax.rsqrt(var + LN_EPS) * clg_ref[...] + clb_ref[...]
        yc_out[r0:r0 + CONV_ROWS, :] = (hn * _sigmoid(hn)).astype(BF16)
    hbuf_ref[0, 0:CONV_HIST, :] = hbuf_ref[0, tm:tm + CONV_HIST, :]


def _mix_in(x2, g, win, mu, w0, wwa, a0, wg, kkp, ka, ones, cw, cb, clg, clb, seq):
    n = x2.shape[0]
    tm = TM_IN
    full = lambda a: pl.BlockSpec(a.shape, lambda i: (0,) * a.ndim)
    tok = lambda w: pl.BlockSpec((tm, w), lambda i: (i, 0))
    consts = (g, win, mu, w0, wwa, a0, wg, kkp, ka, ones, cw, cb, clg, clb)
    out_f32 = jax.ShapeDtypeStruct((n, D_RWKV), F32)
    return pl.pallas_call(
        functools.partial(_mix_in_body, seq // tm),
        grid=(n // tm,),
        in_specs=[tok(D_MODEL)] + [full(a) for a in consts],
        out_specs=[tok(D_RWKV)] * 7 + [tok(D_CONV)],
        out_shape=[out_f32] * 7 + [jax.ShapeDtypeStruct((n, D_CONV), BF16)],
        scratch_shapes=[pltpu.VMEM((SUBLANES, D_RWKV_IN), F32),
                        pltpu.VMEM((SUBLANES, CONV_HIST + tm, D_CONV), F32)],
        compiler_params=pltpu.CompilerParams(
            dimension_semantics=("arbitrary",), vmem_limit_bytes=VMEM_LIMIT),
        name="mix_in",
    )(x2, *consts)


def _rwkv_rec_body(r_ref, lw_ref, k_ref, v_ref, kk_ref, b_ref, g_ref,
                   rk_ref, lng_ref, lnb_ref, ones_ref,
                   o_ref, st_ref, y_ref):
    L = CHUNK
    nch = r_ref.shape[0] // L

    @pl.when(pl.program_id(1) == 0)
    def _():
        st_ref[...] = jnp.zeros_like(st_ref)

    ri = lax.broadcasted_iota(jnp.int32, (L, L), 0)
    ci = lax.broadcasted_iota(jnp.int32, (L, L), 1)
    tri = jnp.where(ri >= ci, 1.0, 0.0).astype(BF16)
    last_row = lax.broadcasted_iota(jnp.int32, (L, D_RWKV), 0) == L - 1
    lane = lax.broadcasted_iota(jnp.int32, (2 * L, LANES), 1)
    rowi = lax.broadcasted_iota(jnp.int32, (2 * L, LANES), 0)
    head_mask2 = (lane < HEAD, lane >= HEAD)
    t_idx = rowi & (L - 1)
    j_idx = lane & (L - 1)
    tri_mask = (j_idx < t_idx) | ((rowi >= L) & (j_idx == t_idx))
    lane1 = lax.broadcasted_iota(jnp.int32, (L, LANES), 1)
    row1 = lax.broadcasted_iota(jnp.int32, (L, LANES), 0)
    head_mask1 = (lane1 < HEAD, lane1 >= HEAD)
    diag_mask = row1 == lane1
    eye_pair = jnp.where(row1 == (lane1 & (L - 1)), 1.0, 0.0)
    zeros_b = jnp.zeros((L, LANES), BF16)
    pairs = range(N_HEADS // 2)
    sls = [slice(pr * LANES, (pr + 1) * LANES) for pr in pairs]

    q_p, c_p, d_t, q_roll, v_p, v_roll, pe_p, pe_roll = ({} for _ in range(8))
    for cc in range(nch):
        rows = pl.ds(cc * L, L)
        lw = lw_ref[rows, :]
        lw_hi = lw.astype(BF16)
        rem = lw - lw_hi.astype(F32)
        lw_mid = rem.astype(BF16)
        lw_lo = (rem - lw_mid.astype(F32)).astype(BF16)
        lp = _dot(tri, lw_hi) + _dot(tri, lw_mid) + _dot(tri, lw_lo)
        lp_end = jnp.sum(jnp.where(last_row, lp, 0.0), axis=0, keepdims=True)
        k = k_ref[rows, :]
        v = v_ref[rows, :]
        b = b_ref[rows, :]
        e_neg = jnp.exp(-lp)
        e_end = jnp.exp(lp_end - lp)
        at = -(kk_ref[rows, :] * jnp.exp(lp - lw))
        rt = r_ref[rows, :] * jnp.exp(lp)
        bt = b * e_neg
        kt = k * e_neg
        bp = b * e_end
        kp = k * e_end
        p_end = jnp.exp(lp_end)
        for pr in pairs:
            sl = sls[pr]
            u = (cc, pr)
            q_p[u] = jnp.concatenate([at[:, sl], rt[:, sl]], axis=0)
            c_p[u] = jnp.concatenate([bt[:, sl], kt[:, sl]], axis=0).astype(BF16)
            d_t[u] = jnp.concatenate([bp[:, sl], kp[:, sl]], axis=0).T.astype(BF16)
            q_roll[u] = pltpu.roll(q_p[u], HEAD, axis=1)
            v_p[u] = v[:, sl]
            v_roll[u] = pltpu.roll(v_p[u], HEAD, axis=1)
            pe_p[u] = p_end[:, sl]
            pe_roll[u] = pltpu.roll(pe_p[u], HEAD, axis=1)

    units = [(cc, pr, hh) for cc in range(nch) for pr in pairs for hh in range(2)]
    a_top, a_bot, q_lo, v_hi, xw, npow = [], [], [], [], [], []
    for cc, pr, hh in units:
        u = (cc, pr)
        a_all = _dot_nt(jnp.where(head_mask2[hh], q_p[u], 0.0).astype(BF16), c_p[u])
        a_all = jnp.where(tri_mask, a_all, 0.0).astype(BF16)
        a_top.append(a_all[:L])
        a_bot.append(a_all[L:])
        q_lo.append(jnp.where(head_mask2[0], q_p[u] if hh == 0 else q_roll[u], 0.0))
        v_hi.append(jnp.where(head_mask1[1], v_p[u] if hh == 1 else v_roll[u], 0.0).astype(BF16))
    for i in range(len(units)):
        zv = jnp.concatenate([zeros_b, v_hi[i]], axis=0)
        xw.append(q_lo[i][:L] + _dot(a_top[i], zv))
        npow.append(a_top[i])
    for it in range(6):
        for i in range(len(units)):
            xb = xw[i].astype(BF16)
            if it < 5:
                rhs = jnp.concatenate(
                    [jnp.concatenate([xb, npow[i]], axis=1),
                     jnp.zeros((L, 2 * LANES), BF16)], axis=0)
                res = _dot(npow[i], rhs)
                xw[i] = xw[i] + res[:, :LANES]
                npow[i] = res[:, LANES:].astype(BF16)
            else:
                xw[i] = xw[i] + _dot(npow[i], jnp.concatenate([xb, zeros_b], axis=0))
    lhs2 = {}
    for i, (cc, pr, hh) in enumerate(units):
        u = (cc, pr)
        m2 = jnp.concatenate([xw[i].astype(BF16), v_hi[i]], axis=0)
        ry = _dot(a_bot[i], m2) + q_lo[i][L:]
        gh = _dot(d_t[u][hh * HEAD:(hh + 1) * HEAD, :], m2)
        gh = gh + jnp.where(diag_mask, pe_p[u] if hh == 0 else pe_roll[u], 0.0)
        lhs2[(cc, pr, hh)] = jnp.concatenate([ry, gh], axis=0).astype(BF16)

    eye_rows = [jnp.where(head_mask1[hh], eye_pair, 0.0).astype(BF16) for hh in range(2)]
    st = [st_ref[pr] for pr in pairs]
    for cc in range(nch):
        for pr in pairs:
            rhs2 = jnp.concatenate(
                [jnp.where(head_mask1[0], st[pr], 0.0).astype(BF16), eye_rows[0],
                 jnp.where(head_mask1[1], st[pr], 0.0).astype(BF16), eye_rows[1]], axis=0)
            out = _dot(jnp.concatenate([lhs2[(cc, pr, 0)], lhs2[(cc, pr, 1)]], axis=1), rhs2)
            y_ref[pl.ds(cc * L, L), sls[pr]] = out[:L]
            st[pr] = out[L:]
    for pr in pairs:
        st_ref[pr] = st[pr]

    y = y_ref[...]
    ones = ones_ref[...]
    inv_n = 1.0 / HEAD
    mu = _dot(y.astype(BF16), ones) * inv_n
    d = y - mu
    var = _dot((d * d).astype(BF16), ones) * inv_n
    yn = d * lax.rsqrt(var + GN_EPS) * lng_ref[...] + lnb_ref[...]
    bonus = _dot((r_ref[...] * k_ref[...] * rk_ref[...]).astype(BF16), ones) * v_ref[...]
    o_ref[...] = ((yn + bonus) * g_ref[...]).astype(BF16)


def _rwkv_rec(r, lw, k, v, kk, b, g, rk, lng, lnb, ones, batch, seq):
    rows = REC_CHUNKS * CHUNK
    nstep = seq // rows
    tok = pl.BlockSpec((rows, D_RWKV), lambda bi, ci: (bi * nstep + ci, 0))
    full = lambda a: pl.BlockSpec(a.shape, lambda bi, ci: (0,) * a.ndim)
    consts = (rk, lng, lnb, ones)
    return pl.pallas_call(
        _rwkv_rec_body,
        grid=(batch, nstep),
        in_specs=[tok] * 7 + [full(a) for a in consts],
        out_specs=tok,
        out_shape=jax.ShapeDtypeStruct((batch * seq, D_RWKV), BF16),
        scratch_shapes=[pltpu.VMEM((N_HEADS // 2, HEAD, LANES), F32),
                        pltpu.VMEM((rows, D_RWKV), F32)],
        compiler_params=pltpu.CompilerParams(
            dimension_semantics=("arbitrary", "arbitrary"), vmem_limit_bytes=VMEM_LIMIT),
        name="rwkv_rec",
    )(r, lw, k, v, kk, b, g, *consts)


def _rms(x, g):
    ms = jnp.mean(x * x, axis=-1, keepdims=True)
    return x * lax.rsqrt(ms + RMS_EPS) * g


def _mix_out_body(x_ref, yr_ref, yc_ref, wo_ref, g2_ref, w1_ref, w2_ref, gf_ref, o_ref):
    y_cat = jnp.concatenate([yr_ref[...], yc_ref[...]], axis=1)
    x1 = x_ref[...] + _dot(y_cat, wo_ref[...])
    h = _rms(x1, g2_ref[...]).astype(BF16)
    acc = x1
    cols = D_FF // FF_SPLIT
    for c in range(FF_SPLIT):
        z = _dot(h, w1_ref[:, c * cols:(c + 1) * cols])
        z = jnp.square(jnp.maximum(z, 0.0)).astype(BF16)
        acc = acc + _dot(z, w2_ref[c * cols:(c + 1) * cols, :])
    o_ref[...] = _rms(acc, gf_ref[...])


def _mix_out(x2, yr, yc, wo, g2, w1, w2, gf):
    n = x2.shape[0]
    tm = TM_OUT
    tok = lambda w: pl.BlockSpec((tm, w), lambda i: (i, 0))
    full = lambda a: pl.BlockSpec(a.shape, lambda i: (0,) * a.ndim)
    consts = (wo, g2, w1, w2, gf)
    return pl.pallas_call(
        _mix_out_body,
        grid=(n // tm,),
        in_specs=[tok(D_MODEL), tok(D_RWKV), tok(D_CONV)] + [full(a) for a in consts],
        out_specs=tok(D_MODEL),
        out_shape=jax.ShapeDtypeStruct((n, D_MODEL), F32),
        compiler_params=pltpu.CompilerParams(
            dimension_semantics=("arbitrary",), vmem_limit_bytes=VMEM_LIMIT),
        name="mix_out",
    )(x2, yr, yc, *consts)


def kernel(x, norm_mix_g, w_in, shift_mu, w0, w_decay_up, a0, w_aaa_up, w_gate_up, k_k, k_a, r_k, ln_x_g, ln_x_b, conv_w, conv_b, conv_ln_g, conv_ln_b, w_out, norm_mlp_g, w_ff1, w_ff2, norm_final_g):
    batch, seq, _ = x.shape
    assert norm_mix_g.shape[0] == 1, "single layer"
    assert seq % TM_IN == 0 and seq % (REC_CHUNKS * CHUNK) == 0 and (batch * seq) % TM_OUT == 0
    row = lambda a: a.reshape(1, -1)
    x2 = x.reshape(batch * seq, D_MODEL)

    zer = jnp.zeros((D_DECAY_LORA, D_RWKV), F32)
    wwa = jnp.concatenate(
        [jnp.concatenate([w_decay_up[0], zer], axis=1),
         jnp.concatenate([zer, w_aaa_up[0]], axis=1)], axis=0).astype(BF16)
    hid = jnp.arange(D_RWKV) // HEAD
    ones = (hid[:, None] == hid[None, :]).astype(BF16)

    r, lw, k, v, kk, b, g, yc = _mix_in(
        x2, row(norm_mix_g[0]), w_in[0].astype(BF16), row(shift_mu[0]), row(w0[0]), wwa,
        row(a0[0]), w_gate_up[0].astype(BF16), row(k_k[0]), row(k_a[0]), ones,
        jnp.broadcast_to(conv_w[0][:, None, :], (CONV_WIDTH, SUBLANES, D_CONV)),
        row(conv_b[0]), row(conv_ln_g[0]), row(conv_ln_b[0]), seq)
    yr = _rwkv_rec(r, lw, k, v, kk, b, g, row(r_k[0]), row(ln_x_g[0]), row(ln_x_b[0]), ones,
                   batch, seq)
    out = _mix_out(x2, yr, yc, w_out[0].astype(BF16), row(norm_mlp_g[0]),
                   w_ff1[0].astype(BF16), w_ff2[0].astype(BF16), row(norm_final_g))
    return out.reshape(batch, seq, D_MODEL)
```

```python
import functools

import jax
import jax.numpy as jnp
from jax import lax
from jax.experimental import pallas as pl
from jax.experimental.pallas import tpu as pltpu

F32 = jnp.float32
BF16 = jnp.bfloat16

D_MODEL = 1024
D_RWKV = 512
HEAD = 64
N_HEADS = D_RWKV // HEAD
D_CONV = 512
CONV_WIDTH = 31
D_DECAY_LORA = 64
D_AAA_LORA = 64
D_GATE_LORA = 128
D_FF = 4 * D_MODEL
D_RWKV_IN = 3 * D_RWKV + D_DECAY_LORA + D_AAA_LORA + D_GATE_LORA
D_IN = D_RWKV_IN + 2 * D_CONV
RMS_EPS = 1e-6
GN_EPS = 64e-5
LN_EPS = 1e-5
KK_NORM_FLOOR = 1e-12
DECAY_SCALE = 0.6065306597126334

LANES = 128
SUBLANES = 8
CHUNK = 64
REC_CHUNKS = 8
CONV_HIST = 32
CONV_ROWS = 32
TM_IN = 512
TM_OUT = 1024
FF_SPLIT = 4
VMEM_LIMIT = 56 * 1024 * 1024


def _sigmoid(z):
    return 1.0 / (1.0 + jnp.exp(-z))


def _dot(a, b):
    return jnp.dot(a, b, preferred_element_type=F32)


def _dot_nt(a, b):
    return lax.dot_general(a, b, (((1,), (1,)), ((), ())), preferred_element_type=F32)


def _mix_in_body(tiles_per_seq,
                 x_ref, g_ref, win_ref, mu_ref, w0_ref, wwa_ref, a0_ref, wg_ref,
                 kkp_ref, ka_ref, ones_ref, cw_ref, cb_ref, clg_ref, clb_ref,
                 r_out, lw_out, k_out, v_out, kk_out, b_out, g_out, yc_out,
                 carry_ref, hbuf_ref):
    tm = x_ref.shape[0]

    @pl.when(pl.program_id(0) % tiles_per_seq == 0)
    def _():
        carry_ref[...] = jnp.zeros_like(carry_ref)
        hbuf_ref[0, 0:CONV_HIST, :] = jnp.zeros((CONV_HIST, D_CONV), F32)

    x = x_ref[...]
    ms = jnp.mean(x * x, axis=-1, keepdims=True)
    h = (x * lax.rsqrt(ms + RMS_EPS) * g_ref[...]).astype(BF16)
    p = _dot(h, win_ref[...])

    pr = p[:, :D_RWKV_IN]
    row = lax.broadcasted_iota(jnp.int32, pr.shape, 0)
    prev = jnp.where(row == 0, carry_ref[7:8, :], pltpu.roll(pr, 1, axis=0))
    carry_ref[...] = pr[tm - 8:tm, :]
    pm = pr + (prev - pr) * mu_ref[...]

    r = pm[:, 0:D_RWKV]
    k = pm[:, D_RWKV:2 * D_RWKV]
    v = pm[:, 2 * D_RWKV:3 * D_RWKV]
    lora_in = pm[:, 3 * D_RWKV:3 * D_RWKV + LANES]
    g_down = pm[:, 3 * D_RWKV + LANES:D_RWKV_IN]

    lane = lax.broadcasted_iota(jnp.int32, lora_in.shape, 1)
    lora_act = jnp.where(lane < D_DECAY_LORA, jnp.tanh(lora_in), lora_in).astype(BF16)
    wa = _dot(lora_act, wwa_ref[...])
    lw_out[...] = -DECAY_SCALE * _sigmoid(w0_ref[...] + wa[:, :D_RWKV])
    a_sig = _sigmoid(a0_ref[...] + wa[:, D_RWKV:])
    g_out[...] = _dot(_sigmoid(g_down).astype(BF16), wg_ref[...])

    kk = k * kkp_ref[...]
    ss = _dot((kk * kk).astype(BF16), ones_ref[...])
    kk = kk * lax.rsqrt(jnp.maximum(ss, KK_NORM_FLOOR * KK_NORM_FLOOR))
    r_out[...] = r
    k_out[...] = k * (1.0 + (a_sig - 1.0) * ka_ref[...])
    v_out[...] = v
    kk_out[...] = kk
    b_out[...] = kk * a_sig

    u1 = p[:, D_RWKV_IN:D_RWKV_IN + D_CONV]
    u2 = p[:, D_RWKV_IN + D_CONV:D_IN]
    hbuf_ref[0, CONV_HIST:CONV_HIST + tm, :] = u1 * _sigmoid(u2)
    span = tm + CONV_HIST - SUBLANES
    for s in range(1, SUBLANES):
        hbuf_ref[s, 0:span, :] = hbuf_ref[0, pl.ds(s, span), :]
    base = CONV_HIST - (CONV_WIDTH - 1)
    for r0 in range(0, tm, CONV_ROWS):
        acc = jnp.broadcast_to(cb_ref[...], (CONV_ROWS, D_CONV))
        for j in range(CONV_WIDTH):
            off = base + j
            wj = jnp.concatenate([cw_ref[j]] * (CONV_ROWS // SUBLANES), axis=0)
            acc = acc + wj * hbuf_ref[off % SUBLANES, pl.ds(r0 + off - off % SUBLANES, CONV_ROWS), :]
        mu = jnp.mean(acc, axis=-1, keepdims=True)
        d = acc - mu
        var = jnp.mean(d * d, axis=-1, keepdims=True)
        hn = d * lax.rsqrt(var + LN_EPS) * clg_ref[...] + clb_ref[...]
        yc_out[r0:r0 + CONV_ROWS, :] = (hn * _sigmoid(hn)).astype(BF16)
    hbuf_ref[0, 0:CONV_HIST, :] = hbuf_ref[0, tm:tm + CONV_HIST, :]


def _mix_in(x2, g, win, mu, w0, wwa, a0, wg, kkp, ka, ones, cw, cb, clg, clb, seq):
    n = x2.shape[0]
    tm = TM_IN
    full = lambda a: pl.BlockSpec(a.shape, lambda i: (0,) * a.ndim, pipeline_mode=pl.Buffered(1))
    tok = lambda w: pl.BlockSpec((tm, w), lambda i: (i, 0))
    consts = (g, win, mu, w0, wwa, a0, wg, kkp, ka, ones, cw, cb, clg, clb)
    out_f32 = jax.ShapeDtypeStruct((n, D_RWKV), F32)
    return pl.pallas_call(
        functools.partial(_mix_in_body, seq // tm),
        grid=(n // tm,),
        in_specs=[tok(D_MODEL)] + [full(a) for a in consts],
        out_specs=[tok(D_RWKV)] * 7 + [tok(D_CONV)],
        out_shape=[out_f32] * 7 + [jax.ShapeDtypeStruct((n, D_CONV), BF16)],
        scratch_shapes=[pltpu.VMEM((SUBLANES, D_RWKV_IN), F32),
                        pltpu.VMEM((SUBLANES, CONV_HIST + tm, D_CONV), F32)],
        compiler_params=pltpu.CompilerParams(
            dimension_semantics=("arbitrary",), vmem_limit_bytes=VMEM_LIMIT),
        name="mix_in",
    )(x2, *consts)


def _rwkv_rec_body(r_ref, lw_ref, k_ref, v_ref, kk_ref, b_ref, g_ref,
                   rk_ref, lng_ref, lnb_ref, ones_ref,
                   o_ref, st_ref, y_ref):
    L = CHUNK
    nch = r_ref.shape[0] // L

    @pl.when(pl.program_id(1) == 0)
    def _():
        st_ref[...] = jnp.zeros_like(st_ref)

    ri = lax.broadcasted_iota(jnp.int32, (L, L), 0)
    ci = lax.broadcasted_iota(jnp.int32, (L, L), 1)
    tri = jnp.where(ri >= ci, 1.0, 0.0).astype(BF16)
    last_row = lax.broadcasted_iota(jnp.int32, (L, D_RWKV), 0) == L - 1
    lane = lax.broadcasted_iota(jnp.int32, (2 * L, LANES), 1)
    rowi = lax.broadcasted_iota(jnp.int32, (2 * L, LANES), 0)
    head_mask2 = (lane < HEAD, lane >= HEAD)
    t_idx = rowi & (L - 1)
    j_idx = lane & (L - 1)
    tri_mask = (j_idx < t_idx) | ((rowi >= L) & (j_idx == t_idx))
    lane1 = lax.broadcasted_iota(jnp.int32, (L, LANES), 1)
    row1 = lax.broadcasted_iota(jnp.int32, (L, LANES), 0)
    head_mask1 = (lane1 < HEAD, lane1 >= HEAD)
    diag_mask = row1 == lane1
    eye_pair = jnp.where(row1 == (lane1 & (L - 1)), 1.0, 0.0)
    zeros_b = jnp.zeros((L, LANES), BF16)
    pairs = range(N_HEADS // 2)
    sls = [slice(pr * LANES, (pr + 1) * LANES) for pr in pairs]

    q_p, c_p, d_t, q_roll, v_p, v_roll, pe_p, pe_roll = ({} for _ in range(8))
    for cc in range(nch):
        rows = pl.ds(cc * L, L)
        lw = lw_ref[rows, :]
        lw_hi = lw.astype(BF16)
        rem = lw - lw_hi.astype(F32)
        lw_mid = rem.astype(BF16)
        lw_lo = (rem - lw_mid.astype(F32)).astype(BF16)
        lp = _dot(tri, lw_hi) + _dot(tri, lw_mid) + _dot(tri, lw_lo)
        lp_end = jnp.sum(jnp.where(last_row, lp, 0.0), axis=0, keepdims=True)
        k = k_ref[rows, :]
        v = v_ref[rows, :]
        b = b_ref[rows, :]
        e_neg = jnp.exp(-lp)
        e_end = jnp.exp(lp_end - lp)
        at = -(kk_ref[rows, :] * jnp.exp(lp - lw))
        rt = r_ref[rows, :] * jnp.exp(lp)
        bt = b * e_neg
        kt = k * e_neg
        bp = b * e_end
        kp = k * e_end
        p_end = jnp.exp(lp_end)
        for pr in pairs:
            sl = sls[pr]
            u = (cc, pr)
            q_p[u] = jnp.concatenate([at[:, sl], rt[:, sl]], axis=0)
            c_p[u] = jnp.concatenate([bt[:, sl], kt[:, sl]], axis=0).astype(BF16)
            d_t[u] = jnp.concatenate([bp[:, sl], kp[:, sl]], axis=0).T.astype(BF16)
            q_roll[u] = pltpu.roll(q_p[u], HEAD, axis=1)
            v_p[u] = v[:, sl]
            v_roll[u] = pltpu.roll(v_p[u], HEAD, axis=1)
            pe_p[u] = p_end[:, sl]
            pe_roll[u] = pltpu.roll(pe_p[u], HEAD, axis=1)

    units = [(cc, pr, hh) for cc in range(nch) for pr in pairs for hh in range(2)]
    a_top, a_bot, q_lo, v_hi, xw, npow = [], [], [], [], [], []
    for cc, pr, hh in units:
        u = (cc, pr)
        a_all = _dot_nt(jnp.where(head_mask2[hh], q_p[u], 0.0).astype(BF16), c_p[u])
        a_all = jnp.where(tri_mask, a_all, 0.0).astype(BF16)
        a_top.append(a_all[:L])
        a_bot.append(a_all[L:])
        q_lo.append(jnp.where(head_mask2[0], q_p[u] if hh == 0 else q_roll[u], 0.0))
        v_hi.append(jnp.where(head_mask1[1], v_p[u] if hh == 1 else v_roll[u], 0.0).astype(BF16))
    for i in range(len(units)):
        zv = jnp.concatenate([zeros_b, v_hi[i]], axis=0)
        xw.append(q_lo[i][:L] + _dot(a_top[i], zv))
        npow.append(a_top[i])
    for it in range(6):
        for i in range(len(units)):
            xb = xw[i].astype(BF16)
            if it < 5:
                rhs = jnp.concatenate(
                    [jnp.concatenate([xb, npow[i]], axis=1),
                     jnp.zeros((L, 2 * LANES), BF16)], axis=0)
                res = _dot(npow[i], rhs)
                xw[i] = xw[i] + res[:, :LANES]
                npow[i] = res[:, LANES:].astype(BF16)
            else:
                xw[i] = xw[i] + _dot(npow[i], jnp.concatenate([xb, zeros_b], axis=0))
    lhs2 = {}
    for i, (cc, pr, hh) in enumerate(units):
        u = (cc, pr)
        m2 = jnp.concatenate([xw[i].astype(BF16), v_hi[i]], axis=0)
        ry = _dot(a_bot[i], m2) + q_lo[i][L:]
        gh = _dot(d_t[u][hh * HEAD:(hh + 1) * HEAD, :], m2)
        gh = gh + jnp.where(diag_mask, pe_p[u] if hh == 0 else pe_roll[u], 0.0)
        lhs2[(cc, pr, hh)] = jnp.concatenate([ry, gh], axis=0).astype(BF16)

    eye_rows = [jnp.where(head_mask1[hh], eye_pair, 0.0).astype(BF16) for hh in range(2)]
    st = [st_ref[pr] for pr in pairs]
    for cc in range(nch):
        for pr in pairs:
            rhs2 = jnp.concatenate(
                [jnp.where(head_mask1[0], st[pr], 0.0).astype(BF16), eye_rows[0],
                 jnp.where(head_mask1[1], st[pr], 0.0).astype(BF16), eye_rows[1]], axis=0)
            out = _dot(jnp.concatenate([lhs2[(cc, pr, 0)], lhs2[(cc, pr, 1)]], axis=1), rhs2)
            y_ref[pl.ds(cc * L, L), sls[pr]] = out[:L]
            st[pr] = out[L:]
    for pr in pairs:
        st_ref[pr] = st[pr]

    y = y_ref[...]
    ones = ones_ref[...]
    inv_n = 1.0 / HEAD
    mu = _dot(y.astype(BF16), ones) * inv_n
    d = y - mu
    var = _dot((d * d).astype(BF16), ones) * inv_n
    yn = d * lax.rsqrt(var + GN_EPS) * lng_ref[...] + lnb_ref[...]
    bonus = _dot((r_ref[...] * k_ref[...] * rk_ref[...]).astype(BF16), ones) * v_ref[...]
    o_ref[...] = ((yn + bonus) * g_ref[...]).astype(BF16)


def _rwkv_rec(r, lw, k, v, kk, b, g, rk, lng, lnb, ones, batch, seq):
    rows = REC_CHUNKS * CHUNK
    nstep = seq // rows
    tok = pl.BlockSpec((rows, D_RWKV), lambda bi, ci: (bi * nstep + ci, 0))
    full = lambda a: pl.BlockSpec(a.shape, lambda bi, ci: (0,) * a.ndim)
    consts = (rk, lng, lnb, ones)
    return pl.pallas_call(
        _rwkv_rec_body,
        grid=(batch, nstep),
        in_specs=[tok] * 7 + [full(a) for a in consts],
        out_specs=tok,
        out_shape=jax.ShapeDtypeStruct((batch * seq, D_RWKV), BF16),
        scratch_shapes=[pltpu.VMEM((N_HEADS // 2, HEAD, LANES), F32),
                        pltpu.VMEM((rows, D_RWKV), F32)],
        compiler_params=pltpu.CompilerParams(
            dimension_semantics=("arbitrary", "arbitrary"), vmem_limit_bytes=VMEM_LIMIT),
        name="rwkv_rec",
    )(r, lw, k, v, kk, b, g, *consts)


def _rms(x, g):
    ms = jnp.mean(x * x, axis=-1, keepdims=True)
    return x * lax.rsqrt(ms + RMS_EPS) * g


def _mix_out_body(x_ref, yr_ref, yc_ref, wo_ref, g2_ref, w1_ref, w2_ref, gf_ref, o_ref):
    y_cat = jnp.concatenate([yr_ref[...], yc_ref[...]], axis=1)
    x1 = x_ref[...] + _dot(y_cat, wo_ref[...])
    h = _rms(x1, g2_ref[...]).astype(BF16)
    acc = x1
    cols = D_FF // FF_SPLIT
    for c in range(FF_SPLIT):
        z = _dot(h, w1_ref[:, c * cols:(c + 1) * cols])
        z = jnp.square(jnp.maximum(z, 0.0)).astype(BF16)
        acc = acc + _dot(z, w2_ref[c * cols:(c + 1) * cols, :])
    o_ref[...] = _rms(acc, gf_ref[...])


def _mix_out(x2, yr, yc, wo, g2, w1, w2, gf):
    n = x2.shape[0]
    tm = TM_OUT
    tok = lambda w: pl.BlockSpec((tm, w), lambda i: (i, 0))
    full = lambda a: pl.BlockSpec(a.shape, lambda i: (0,) * a.ndim, pipeline_mode=pl.Buffered(1))
    consts = (wo, g2, w1, w2, gf)
    return pl.pallas_call(
        _mix_out_body,
        grid=(n // tm,),
        in_specs=[tok(D_MODEL), tok(D_RWKV), tok(D_CONV)] + [full(a) for a in consts],
        out_specs=tok(D_MODEL),
        out_shape=jax.ShapeDtypeStruct((n, D_MODEL), F32),
        compiler_params=pltpu.CompilerParams(
            dimension_semantics=("arbitrary",), vmem_limit_bytes=VMEM_LIMIT),
        name="mix_out",
    )(x2, yr, yc, *consts)


def kernel(x, norm_mix_g, w_in, shift_mu, w0, w_decay_up, a0, w_aaa_up, w_gate_up, k_k, k_a, r_k, ln_x_g, ln_x_b, conv_w, conv_b, conv_ln_g, conv_ln_b, w_out, norm_mlp_g, w_ff1, w_ff2, norm_final_g):
    batch, seq, _ = x.shape
    assert norm_mix_g.shape[0] == 1, "single layer"
    assert seq % TM_IN == 0 and seq % (REC_CHUNKS * CHUNK) == 0 and (batch * seq) % TM_OUT == 0
    row = lambda a: a.reshape(1, -1)
    x2 = x.reshape(batch * seq, D_MODEL)

    zer = jnp.zeros((D_DECAY_LORA, D_RWKV), F32)
    wwa = jnp.concatenate(
        [jnp.concatenate([w_decay_up[0], zer], axis=1),
         jnp.concatenate([zer, w_aaa_up[0]], axis=1)], axis=0).astype(BF16)
    hid = jnp.arange(D_RWKV) // HEAD
    ones = (hid[:, None] == hid[None, :]).astype(BF16)

    r, lw, k, v, kk, b, g, yc = _mix_in(
        x2, row(norm_mix_g[0]), w_in[0].astype(BF16), row(shift_mu[0]), row(w0[0]), wwa,
        row(a0[0]), w_gate_up[0].astype(BF16), row(k_k[0]), row(k_a[0]), ones,
        jnp.broadcast_to(conv_w[0][:, None, :], (CONV_WIDTH, SUBLANES, D_CONV)),
        row(conv_b[0]), row(conv_ln_g[0]), row(conv_ln_b[0]), seq)
    yr = _rwkv_rec(r, lw, k, v, kk, b, g, row(r_k[0]), row(ln_x_g[0]), row(ln_x_b[0]), ones,
                   batch, seq)
    out = _mix_out(x2, yr, yc, w_out[0].astype(BF16), row(norm_mlp_g[0]),
                   w_ff1[0].astype(BF16), w_ff2[0].astype(BF16), row(norm_final_g))
    return out.reshape(batch, seq, D_MODEL)
```

```python
import functools

import jax
import jax.numpy as jnp
from jax import lax
from jax.experimental import pallas as pl
from jax.experimental.pallas import tpu as pltpu

F32 = jnp.float32
BF16 = jnp.bfloat16

D_MODEL = 1024
D_RWKV = 512
HEAD = 64
N_HEADS = D_RWKV // HEAD
D_CONV = 512
CONV_WIDTH = 31
D_DECAY_LORA = 64
D_AAA_LORA = 64
D_GATE_LORA = 128
D_FF = 4 * D_MODEL
D_RWKV_IN = 3 * D_RWKV + D_DECAY_LORA + D_AAA_LORA + D_GATE_LORA
D_IN = D_RWKV_IN + 2 * D_CONV
RMS_EPS = 1e-6
GN_EPS = 64e-5
LN_EPS = 1e-5
KK_NORM_FLOOR = 1e-12
DECAY_SCALE = 0.6065306597126334

LANES = 128
SUBLANES = 8
CHUNK = 64
REC_CHUNKS = 8
CONV_HIST = 32
CONV_ROWS = 32
TM_IN = 512
TM_OUT = 1024
FF_SPLIT = 4
VMEM_LIMIT = 56 * 1024 * 1024


def _sigmoid(z):
    return 1.0 / (1.0 + jnp.exp(-z))


def _dot(a, b):
    return jnp.dot(a, b, preferred_element_type=F32)


def _dot_nt(a, b):
    return lax.dot_general(a, b, (((1,), (1,)), ((), ())), preferred_element_type=F32)


def _mix_in_body(tiles_per_seq,
                 x_ref, g_ref, win_ref, mu_ref, w0_ref, wwa_ref, a0_ref, wg_ref,
                 kkp_ref, ka_ref, ones_ref, cw_ref, cb_ref, clg_ref, clb_ref,
                 r_out, lw_out, k_out, v_out, kk_out, b_out, g_out, yc_out,
                 carry_ref, hbuf_ref):
    tm = x_ref.shape[0]

    @pl.when(pl.program_id(0) % tiles_per_seq == 0)
    def _():
        carry_ref[...] = jnp.zeros_like(carry_ref)
        hbuf_ref[0, 0:CONV_HIST, :] = jnp.zeros((CONV_HIST, D_CONV), F32)

    x = x_ref[...]
    ms = jnp.mean(x * x, axis=-1, keepdims=True)
    h = (x * lax.rsqrt(ms + RMS_EPS) * g_ref[...]).astype(BF16)
    p = _dot(h, win_ref[...])

    pr = p[:, :D_RWKV_IN]
    rolled = pltpu.roll(pr, 1, axis=0)
    row = lax.broadcasted_iota(jnp.int32, (SUBLANES, D_RWKV_IN), 0)
    first = jnp.where(row == 0, carry_ref[7:8, :], rolled[:SUBLANES])
    prev = jnp.concatenate([first, rolled[SUBLANES:]], axis=0)
    carry_ref[...] = pr[tm - 8:tm, :]
    pm = pr + (prev - pr) * mu_ref[...]

    r = pm[:, 0:D_RWKV]
    k = pm[:, D_RWKV:2 * D_RWKV]
    v = pm[:, 2 * D_RWKV:3 * D_RWKV]
    lora_in = pm[:, 3 * D_RWKV:3 * D_RWKV + LANES]
    g_down = pm[:, 3 * D_RWKV + LANES:D_RWKV_IN]

    lane = lax.broadcasted_iota(jnp.int32, lora_in.shape, 1)
    lora_act = jnp.where(lane < D_DECAY_LORA, jnp.tanh(lora_in), lora_in).astype(BF16)
    wa = _dot(lora_act, wwa_ref[...])
    lw_out[...] = -DECAY_SCALE * _sigmoid(w0_ref[...] + wa[:, :D_RWKV])
    a_sig = _sigmoid(a0_ref[...] + wa[:, D_RWKV:])
    g_out[...] = _dot(_sigmoid(g_down).astype(BF16), wg_ref[...])

    kk = k * kkp_ref[...]
    ss = _dot((kk * kk).astype(BF16), ones_ref[...])
    kk = kk * lax.rsqrt(jnp.maximum(ss, KK_NORM_FLOOR * KK_NORM_FLOOR))
    r_out[...] = r
    k_out[...] = k * (1.0 + (a_sig - 1.0) * ka_ref[...])
    v_out[...] = v
    kk_out[...] = kk
    b_out[...] = kk * a_sig

    u1 = p[:, D_RWKV_IN:D_RWKV_IN + D_CONV]
    u2 = p[:, D_RWKV_IN + D_CONV:D_IN]
    hbuf_ref[0, CONV_HIST:CONV_HIST + tm, :] = u1 * _sigmoid(u2)
    span = tm + CONV_HIST - SUBLANES
    for s in range(1, SUBLANES):
        hbuf_ref[s, 0:span, :] = hbuf_ref[0, pl.ds(s, span), :]
    base = CONV_HIST - (CONV_WIDTH - 1)
    for r0 in range(0, tm, CONV_ROWS):
        acc = jnp.broadcast_to(cb_ref[...], (CONV_ROWS, D_CONV))
        for j in range(CONV_WIDTH):
            off = base + j
            wj = jnp.concatenate([cw_ref[j]] * (CONV_ROWS // SUBLANES), axis=0)
            acc = acc + wj * hbuf_ref[off % SUBLANES, pl.ds(r0 + off - off % SUBLANES, CONV_ROWS), :]
        mu = jnp.mean(acc, axis=-1, keepdims=True)
        d = acc - mu
        var = jnp.mean(d * d, axis=-1, keepdims=True)
        hn = d * lax.rsqrt(var + LN_EPS) * clg_ref[...] + clb_ref[...]
        yc_out[r0:r0 + CONV_ROWS, :] = (hn * _sigmoid(hn)).astype(BF16)
    hbuf_ref[0, 0:CONV_HIST, :] = hbuf_ref[0, tm:tm + CONV_HIST, :]


def _mix_in(x2, g, win, mu, w0, wwa, a0, wg, kkp, ka, ones, cw, cb, clg, clb, seq):
    n = x2.shape[0]
    tm = TM_IN
    full = lambda a: pl.BlockSpec(a.shape, lambda i: (0,) * a.ndim, pipeline_mode=pl.Buffered(1))
    tok = lambda w: pl.BlockSpec((tm, w), lambda i: (i, 0))
    consts = (g, win, mu, w0, wwa, a0, wg, kkp, ka, ones, cw, cb, clg, clb)
    out_f32 = jax.ShapeDtypeStruct((n, D_RWKV), F32)
    return pl.pallas_call(
        functools.partial(_mix_in_body, seq // tm),
        grid=(n // tm,),
        in_specs=[tok(D_MODEL)] + [full(a) for a in consts],
        out_specs=[tok(D_RWKV)] * 7 + [tok(D_CONV)],
        out_shape=[out_f32] * 7 + [jax.ShapeDtypeStruct((n, D_CONV), BF16)],
        scratch_shapes=[pltpu.VMEM((SUBLANES, D_RWKV_IN), F32),
                        pltpu.VMEM((SUBLANES, CONV_HIST + tm, D_CONV), F32)],
        compiler_params=pltpu.CompilerParams(
            dimension_semantics=("arbitrary",), vmem_limit_bytes=VMEM_LIMIT),
        name="mix_in",
    )(x2, *consts)


def _rwkv_rec_body(r_ref, lw_ref, k_ref, v_ref, kk_ref, b_ref, g_ref,
                   rk_ref, lng_ref, lnb_ref, ones_ref,
                   o_ref, st_ref, y_ref):
    L = CHUNK
    nch = r_ref.shape[0] // L

    @pl.when(pl.program_id(1) == 0)
    def _():
        st_ref[...] = jnp.zeros_like(st_ref)

    ri = lax.broadcasted_iota(jnp.int32, (L, L), 0)
    ci = lax.broadcasted_iota(jnp.int32, (L, L), 1)
    tri = jnp.where(ri >= ci, 1.0, 0.0).astype(BF16)
    last_row = lax.broadcasted_iota(jnp.int32, (L, D_RWKV), 0) == L - 1
    lane = lax.broadcasted_iota(jnp.int32, (2 * L, LANES), 1)
    rowi = lax.broadcasted_iota(jnp.int32, (2 * L, LANES), 0)
    head_mask2 = (lane < HEAD, lane >= HEAD)
    t_idx = rowi & (L - 1)
    j_idx = lane & (L - 1)
    tri_mask = (j_idx < t_idx) | ((rowi >= L) & (j_idx == t_idx))
    lane1 = lax.broadcasted_iota(jnp.int32, (L, LANES), 1)
    row1 = lax.broadcasted_iota(jnp.int32, (L, LANES), 0)
    head_mask1 = (lane1 < HEAD, lane1 >= HEAD)
    diag_mask = row1 == lane1
    eye_pair = jnp.where(row1 == (lane1 & (L - 1)), 1.0, 0.0)
    zeros_b = jnp.zeros((L, LANES), BF16)
    pairs = range(N_HEADS // 2)
    sls = [slice(pr * LANES, (pr + 1) * LANES) for pr in pairs]

    q_p, c_p, d_t, q_roll, v_p, v_roll, pe_p, pe_roll = ({} for _ in range(8))
    for cc in range(nch):
        rows = pl.ds(cc * L, L)
        lw = lw_ref[rows, :]
        lw_hi = lw.astype(BF16)
        rem = lw - lw_hi.astype(F32)
        lw_mid = rem.astype(BF16)
        lw_lo = (rem - lw_mid.astype(F32)).astype(BF16)
        lp = _dot(tri, lw_hi) + _dot(tri, lw_mid) + _dot(tri, lw_lo)
        lp_end = jnp.sum(jnp.where(last_row, lp, 0.0), axis=0, keepdims=True)
        k = k_ref[rows, :]
        v = v_ref[rows, :]
        b = b_ref[rows, :]
        e_neg = jnp.exp(-lp)
        e_end = jnp.exp(lp_end - lp)
        at = -(kk_ref[rows, :] * jnp.exp(lp - lw))
        rt = r_ref[rows, :] * jnp.exp(lp)
        bt = b * e_neg
        kt = k * e_neg
        bp = b * e_end
        kp = k * e_end
        p_end = jnp.exp(lp_end)
        for pr in pairs:
            sl = sls[pr]
            u = (cc, pr)
            q_p[u] = jnp.concatenate([at[:, sl], rt[:, sl]], axis=0)
            c_p[u] = jnp.concatenate([bt[:, sl], kt[:, sl]], axis=0).astype(BF16)
            d_t[u] = jnp.concatenate([bp[:, sl], kp[:, sl]], axis=0).T.astype(BF16)
            q_roll[u] = pltpu.roll(q_p[u], HEAD, axis=1)
            v_p[u] = v[:, sl]
            v_roll[u] = pltpu.roll(v_p[u], HEAD, axis=1)
            pe_p[u] = p_end[:, sl]
            pe_roll[u] = pltpu.roll(pe_p[u], HEAD, axis=1)

    units = [(cc, pr, hh) for cc in range(nch) for pr in pairs for hh in range(2)]
    a_top, a_bot, q_lo, v_hi, xw, npow = [], [], [], [], [], []
    zeros_c = jnp.zeros((2 * L, LANES), BF16)
    for cc, pr, hh in units:
        u = (cc, pr)
        if hh == 0:
            c_heads = jnp.concatenate([jnp.where(head_mask2[0], c_p[u], zeros_c),
                                       jnp.where(head_mask2[1], c_p[u], zeros_c)], axis=0)
            a_pair = _dot_nt(q_p[u].astype(BF16), c_heads)
        a_all = a_pair[:, hh * 2 * L:(hh + 1) * 2 * L]
        a_all = jnp.where(tri_mask, a_all, 0.0).astype(BF16)
        a_top.append(a_all[:L])
        a_bot.append(a_all[L:])
        q_lo.append(jnp.where(head_mask2[0], q_p[u] if hh == 0 else q_roll[u], 0.0))
        v_hi.append(jnp.where(head_mask1[1], v_p[u] if hh == 1 else v_roll[u], 0.0).astype(BF16))
    for i in range(len(units)):
        zv = jnp.concatenate([zeros_b, v_hi[i]], axis=0)
        xw.append(q_lo[i][:L] + _dot(a_top[i], zv))
        npow.append(a_top[i])
    for it in range(6):
        lo = {4: L // 4, 5: L // 2}.get(it, 0)
        for i in range(len(units)):
            xb = xw[i].astype(BF16)
            if it < 4:
                rhs = jnp.concatenate(
                    [jnp.concatenate([xb, npow[i]], axis=1),
                     jnp.zeros((L, 2 * LANES), BF16)], axis=0)
                res = _dot(npow[i], rhs)
                xw[i] = xw[i] + res[:, :LANES]
                npow[i] = res[:, LANES:].astype(BF16)
            elif it == 4:
                keep = L - lo
                rhs = jnp.concatenate(
                    [jnp.concatenate([xb[:keep], npow[i][:keep]], axis=1),
                     jnp.zeros((2 * L - keep, 2 * LANES), BF16)], axis=0)
                res = _dot(npow[i][lo:], rhs)
                xw[i] = jnp.concatenate([xw[i][:lo], xw[i][lo:] + res[:, :LANES]], axis=0)
                npow[i] = res[L // 2 - lo:, LANES:].astype(BF16)
            else:
                keep = L - lo
                rhs = jnp.concatenate([xb[:keep], jnp.zeros((2 * L - keep, LANES), BF16)], axis=0)
                res = _dot(npow[i], rhs)
                xw[i] = jnp.concatenate([xw[i][:lo], xw[i][lo:] + res], axis=0)
    lhs2 = {}
    for i, (cc, pr, hh) in enumerate(units):
        u = (cc, pr)
        m2 = jnp.concatenate([xw[i].astype(BF16), v_hi[i]], axis=0)
        both = _dot(jnp.concatenate([a_bot[i], d_t[u][hh * HEAD:(hh + 1) * HEAD, :]], axis=0), m2)
        ry = both[:L] + q_lo[i][L:]
        gh = both[L:]
        gh = gh + jnp.where(diag_mask, pe_p[u] if hh == 0 else pe_roll[u], 0.0)
        lhs2[(cc, pr, hh)] = jnp.concatenate([ry, gh], axis=0).astype(BF16)

    eye_rows = [jnp.where(head_mask1[hh], eye_pair, 0.0).astype(BF16) for hh in range(2)]
    st = [st_ref[pr] for pr in pairs]
    for cc in range(nch):
        for pr in pairs:
            rhs2 = jnp.concatenate(
                [jnp.where(head_mask1[0], st[pr], 0.0).astype(BF16), eye_rows[0],
                 jnp.where(head_mask1[1], st[pr], 0.0).astype(BF16), eye_rows[1]], axis=0)
            out = _dot(jnp.concatenate([lhs2[(cc, pr, 0)], lhs2[(cc, pr, 1)]], axis=1), rhs2)
            y_ref[pl.ds(cc * L, L), sls[pr]] = out[:L]
            st[pr] = out[L:]
    for pr in pairs:
        st_ref[pr] = st[pr]

    y = y_ref[...]
    half = D_RWKV // 2
    ones_half = ones_ref[0:half, 0:half]

    def head_sum(t):
        tb = t.astype(BF16)
        return jnp.concatenate([_dot(tb[:, :half], ones_half), _dot(tb[:, half:], ones_half)], axis=1)

    inv_n = 1.0 / HEAD
    mu = head_sum(y) * inv_n
    d = y - mu
    var = head_sum(d * d) * inv_n
    yn = d * lax.rsqrt(var + GN_EPS) * lng_ref[...] + lnb_ref[...]
    bonus = head_sum(r_ref[...] * k_ref[...] * rk_ref[...]) * v_ref[...]
    o_ref[...] = ((yn + bonus) * g_ref[...]).astype(BF16)


def _rwkv_rec(r, lw, k, v, kk, b, g, rk, lng, lnb, ones, batch, seq):
    rows = REC_CHUNKS * CHUNK
    nstep = seq // rows
    tok = pl.BlockSpec((rows, D_RWKV), lambda bi, ci: (bi * nstep + ci, 0))
    full = lambda a: pl.BlockSpec(a.shape, lambda bi, ci: (0,) * a.ndim)
    consts = (rk, lng, lnb, ones)
    return pl.pallas_call(
        _rwkv_rec_body,
        grid=(batch, nstep),
        in_specs=[tok] * 7 + [full(a) for a in consts],
        out_specs=tok,
        out_shape=jax.ShapeDtypeStruct((batch * seq, D_RWKV), BF16),
        scratch_shapes=[pltpu.VMEM((N_HEADS // 2, HEAD, LANES), F32),
                        pltpu.VMEM((rows, D_RWKV), F32)],
        compiler_params=pltpu.CompilerParams(
            dimension_semantics=("arbitrary", "arbitrary"), vmem_limit_bytes=VMEM_LIMIT),
        name="rwkv_rec",
    )(r, lw, k, v, kk, b, g, *consts)


def _rms(x, g):
    ms = jnp.mean(x * x, axis=-1, keepdims=True)
    return x * lax.rsqrt(ms + RMS_EPS) * g


def _mix_out_body(x_ref, yr_ref, yc_ref, wo_ref, g2_ref, w1_ref, w2_ref, gf_ref, o_ref):
    y_cat = jnp.concatenate([yr_ref[...], yc_ref[...]], axis=1)
    x1 = x_ref[...] + _dot(y_cat, wo_ref[...])
    h = _rms(x1, g2_ref[...]).astype(BF16)
    acc = x1
    cols = D_FF // FF_SPLIT
    for c in range(FF_SPLIT):
        z = _dot(h, w1_ref[:, c * cols:(c + 1) * cols])
        z = jnp.square(jnp.maximum(z, 0.0)).astype(BF16)
        acc = acc + _dot(z, w2_ref[c * cols:(c + 1) * cols, :])
    o_ref[...] = _rms(acc, gf_ref[...])


def _mix_out(x2, yr, yc, wo, g2, w1, w2, gf):
    n = x2.shape[0]
    tm = TM_OUT
    tok = lambda w: pl.BlockSpec((tm, w), lambda i: (i, 0))
    full = lambda a: pl.BlockSpec(a.shape, lambda i: (0,) * a.ndim, pipeline_mode=pl.Buffered(1))
    consts = (wo, g2, w1, w2, gf)
    return pl.pallas_call(
        _mix_out_body,
        grid=(n // tm,),
        in_specs=[tok(D_MODEL), tok(D_RWKV), tok(D_CONV)] + [full(a) for a in consts],
        out_specs=tok(D_MODEL),
        out_shape=jax.ShapeDtypeStruct((n, D_MODEL), F32),
        compiler_params=pltpu.CompilerParams(
            dimension_semantics=("arbitrary",), vmem_limit_bytes=VMEM_LIMIT),
        name="mix_out",
    )(x2, yr, yc, *consts)


def kernel(x, norm_mix_g, w_in, shift_mu, w0, w_decay_up, a0, w_aaa_up, w_gate_up, k_k, k_a, r_k, ln_x_g, ln_x_b, conv_w, conv_b, conv_ln_g, conv_ln_b, w_out, norm_mlp_g, w_ff1, w_ff2, norm_final_g):
    batch, seq, _ = x.shape
    assert norm_mix_g.shape[0] == 1, "single layer"
    assert seq % TM_IN == 0 and seq % (REC_CHUNKS * CHUNK) == 0 and (batch * seq) % TM_OUT == 0
    row = lambda a: a.reshape(1, -1)
    x2 = x.reshape(batch * seq, D_MODEL)

    zer = jnp.zeros((D_DECAY_LORA, D_RWKV), F32)
    wwa = jnp.concatenate(
        [jnp.concatenate([w_decay_up[0], zer], axis=1),
         jnp.concatenate([zer, w_aaa_up[0]], axis=1)], axis=0).astype(BF16)
    hid = jnp.arange(D_RWKV) // HEAD
    ones = (hid[:, None] == hid[None, :]).astype(BF16)

    r, lw, k, v, kk, b, g, yc = _mix_in(
        x2, row(norm_mix_g[0]), w_in[0].astype(BF16), row(shift_mu[0]), row(w0[0]), wwa,
        row(a0[0]), w_gate_up[0].astype(BF16), row(k_k[0]), row(k_a[0]), ones,
        jnp.broadcast_to(conv_w[0][:, None, :], (CONV_WIDTH, SUBLANES, D_CONV)),
        row(conv_b[0]), row(conv_ln_g[0]), row(conv_ln_b[0]), seq)
    yr = _rwkv_rec(r, lw, k, v, kk, b, g, row(r_k[0]), row(ln_x_g[0]), row(ln_x_b[0]), ones,
                   batch, seq)
    out = _mix_out(x2, yr, yc, w_out[0].astype(BF16), row(norm_mlp_g[0]),
                   w_ff1[0].astype(BF16), w_ff2[0].astype(BF16), row(norm_final_g))
    return out.reshape(batch, seq, D_MODEL)
```

```python
import functools

import jax
import jax.numpy as jnp
from jax import lax
from jax.experimental import pallas as pl
from jax.experimental.pallas import tpu as pltpu

F32 = jnp.float32
BF16 = jnp.bfloat16

D_MODEL = 1024
D_RWKV = 512
HEAD = 64
N_HEADS = D_RWKV // HEAD
D_CONV = 512
CONV_WIDTH = 31
D_DECAY_LORA = 64
D_AAA_LORA = 64
D_GATE_LORA = 128
D_FF = 4 * D_MODEL
D_RWKV_IN = 3 * D_RWKV + D_DECAY_LORA + D_AAA_LORA + D_GATE_LORA
D_IN = D_RWKV_IN + 2 * D_CONV
RMS_EPS = 1e-6
GN_EPS = 64e-5
LN_EPS = 1e-5
KK_NORM_FLOOR = 1e-12
DECAY_SCALE = 0.6065306597126334

LANES = 128
SUBLANES = 8
CHUNK = 64
REC_CHUNKS = 2
REC_BATCH = 4
CONV_HIST = 32
CONV_ROWS = 32
TM_IN = 512
TM_OUT = 1024
FF_SPLIT = 4
VMEM_LIMIT = 56 * 1024 * 1024


def _sigmoid(z):
    return 1.0 / (1.0 + jnp.exp(-z))


def _dot(a, b):
    return jnp.dot(a, b, preferred_element_type=F32)


def _dot_nt(a, b):
    return lax.dot_general(a, b, (((1,), (1,)), ((), ())), preferred_element_type=F32)


def _mix_in_body(tiles_per_seq,
                 x_ref, g_ref, win_ref, mu_ref, w0_ref, wwa_ref, a0_ref, wg_ref,
                 kkp_ref, ka_ref, ones_ref, cw_ref, cb_ref, clg_ref, clb_ref,
                 r_out, lw_out, k_out, v_out, kk_out, b_out, g_out, yc_out,
                 carry_ref, hbuf_ref):
    tm = x_ref.shape[0]

    @pl.when(pl.program_id(0) % tiles_per_seq == 0)
    def _():
        carry_ref[...] = jnp.zeros_like(carry_ref)
        hbuf_ref[0, 0:CONV_HIST, :] = jnp.zeros((CONV_HIST, D_CONV), F32)

    x = x_ref[...]
    ms = jnp.mean(x * x, axis=-1, keepdims=True)
    h = (x * lax.rsqrt(ms + RMS_EPS) * g_ref[...]).astype(BF16)
    p = _dot(h, win_ref[...])

    pr = p[:, :D_RWKV_IN]
    rolled = pltpu.roll(pr, 1, axis=0)
    row = lax.broadcasted_iota(jnp.int32, (SUBLANES, D_RWKV_IN), 0)
    first = jnp.where(row == 0, carry_ref[7:8, :], rolled[:SUBLANES])
    prev = jnp.concatenate([first, rolled[SUBLANES:]], axis=0)
    carry_ref[...] = pr[tm - 8:tm, :]
    pm = pr + (prev - pr) * mu_ref[...]

    r = pm[:, 0:D_RWKV]
    k = pm[:, D_RWKV:2 * D_RWKV]
    v = pm[:, 2 * D_RWKV:3 * D_RWKV]
    lora_in = pm[:, 3 * D_RWKV:3 * D_RWKV + LANES]
    g_down = pm[:, 3 * D_RWKV + LANES:D_RWKV_IN]

    lane = lax.broadcasted_iota(jnp.int32, lora_in.shape, 1)
    lora_act = jnp.where(lane < D_DECAY_LORA, jnp.tanh(lora_in), lora_in).astype(BF16)
    wa = _dot(lora_act, wwa_ref[...])
    lw_out[...] = -DECAY_SCALE * _sigmoid(w0_ref[...] + wa[:, :D_RWKV])
    a_sig = _sigmoid(a0_ref[...] + wa[:, D_RWKV:])
    g_out[...] = _dot(_sigmoid(g_down).astype(BF16), wg_ref[...])

    kk = k * kkp_ref[...]
    ss = _dot((kk * kk).astype(BF16), ones_ref[...])
    kk = kk * lax.rsqrt(jnp.maximum(ss, KK_NORM_FLOOR * KK_NORM_FLOOR))
    r_out[...] = r
    k_out[...] = k * (1.0 + (a_sig - 1.0) * ka_ref[...])
    v_out[...] = v
    kk_out[...] = kk
    b_out[...] = kk * a_sig

    u1 = p[:, D_RWKV_IN:D_RWKV_IN + D_CONV]
    u2 = p[:, D_RWKV_IN + D_CONV:D_IN]
    hbuf_ref[0, CONV_HIST:CONV_HIST + tm, :] = u1 * _sigmoid(u2)
    span = tm + CONV_HIST - SUBLANES
    for s in range(1, SUBLANES):
        hbuf_ref[s, 0:span, :] = hbuf_ref[0, pl.ds(s, span), :]
    base = CONV_HIST - (CONV_WIDTH - 1)
    for r0 in range(0, tm, CONV_ROWS):
        acc = jnp.broadcast_to(cb_ref[...], (CONV_ROWS, D_CONV))
        for j in range(CONV_WIDTH):
            off = base + j
            wj = jnp.concatenate([cw_ref[j]] * (CONV_ROWS // SUBLANES), axis=0)
            acc = acc + wj * hbuf_ref[off % SUBLANES, pl.ds(r0 + off - off % SUBLANES, CONV_ROWS), :]
        mu = jnp.mean(acc, axis=-1, keepdims=True)
        d = acc - mu
        var = jnp.mean(d * d, axis=-1, keepdims=True)
        hn = d * lax.rsqrt(var + LN_EPS) * clg_ref[...] + clb_ref[...]
        yc_out[r0:r0 + CONV_ROWS, :] = (hn * _sigmoid(hn)).astype(BF16)
    hbuf_ref[0, 0:CONV_HIST, :] = hbuf_ref[0, tm:tm + CONV_HIST, :]


def _mix_in(x2, g, win, mu, w0, wwa, a0, wg, kkp, ka, ones, cw, cb, clg, clb, seq):
    n = x2.shape[0]
    tm = TM_IN
    full = lambda a: pl.BlockSpec(a.shape, lambda i: (0,) * a.ndim, pipeline_mode=pl.Buffered(1))
    tok = lambda w: pl.BlockSpec((tm, w), lambda i: (i, 0))
    consts = (g, win, mu, w0, wwa, a0, wg, kkp, ka, ones, cw, cb, clg, clb)
    out_f32 = jax.ShapeDtypeStruct((n, D_RWKV), F32)
    return pl.pallas_call(
        functools.partial(_mix_in_body, seq // tm),
        grid=(n // tm,),
        in_specs=[tok(D_MODEL)] + [full(a) for a in consts],
        out_specs=[tok(D_RWKV)] * 7 + [tok(D_CONV)],
        out_shape=[out_f32] * 7 + [jax.ShapeDtypeStruct((n, D_CONV), BF16)],
        scratch_shapes=[pltpu.VMEM((SUBLANES, D_RWKV_IN), F32),
                        pltpu.VMEM((SUBLANES, CONV_HIST + tm, D_CONV), F32)],
        compiler_params=pltpu.CompilerParams(
            dimension_semantics=("arbitrary",), vmem_limit_bytes=VMEM_LIMIT),
        name="mix_in",
    )(x2, *consts)


def _rwkv_rec_body(r_ref, lw_ref, k_ref, v_ref, kk_ref, b_ref, g_ref,
                   rk_ref, lng_ref, lnb_ref, ones_ref,
                   o_ref, st_ref, y_ref):
    L = CHUNK
    nb, rc = r_ref.shape[1], r_ref.shape[3]
    cpb = rc // L
    nch = nb * cpb

    def chunk(ref, vc):
        return ref[0, vc // cpb, 0, pl.ds((vc % cpb) * L, L), :]

    def whole(ref):
        return jnp.concatenate([ref[0, g, 0] for g in range(nb)], axis=0)

    @pl.when(pl.program_id(1) == 0)
    def _():
        st_ref[...] = jnp.zeros_like(st_ref)

    ri = lax.broadcasted_iota(jnp.int32, (L, L), 0)
    ci = lax.broadcasted_iota(jnp.int32, (L, L), 1)
    tri = jnp.where(ri >= ci, 1.0, 0.0).astype(BF16)
    last_row = lax.broadcasted_iota(jnp.int32, (L, D_RWKV), 0) == L - 1
    lane = lax.broadcasted_iota(jnp.int32, (2 * L, LANES), 1)
    rowi = lax.broadcasted_iota(jnp.int32, (2 * L, LANES), 0)
    head_mask2 = (lane < HEAD, lane >= HEAD)
    t_idx = rowi & (L - 1)
    j_idx = lane & (L - 1)
    tri_mask = (j_idx < t_idx) | ((rowi >= L) & (j_idx == t_idx))
    lane1 = lax.broadcasted_iota(jnp.int32, (L, LANES), 1)
    row1 = lax.broadcasted_iota(jnp.int32, (L, LANES), 0)
    head_mask1 = (lane1 < HEAD, lane1 >= HEAD)
    diag_mask = row1 == lane1
    eye_pair = jnp.where(row1 == (lane1 & (L - 1)), 1.0, 0.0)
    zeros_b = jnp.zeros((L, LANES), BF16)
    pairs = range(N_HEADS // 2)
    sls = [slice(pr * LANES, (pr + 1) * LANES) for pr in pairs]

    q_p, c_p, d_t, q_roll, v_p, v_roll, pe_p, pe_roll = ({} for _ in range(8))
    for cc in range(nch):
        lw = chunk(lw_ref, cc)
        lw_hi = lw.astype(BF16)
        rem = lw - lw_hi.astype(F32)
        lw_mid = rem.astype(BF16)
        lw_lo = (rem - lw_mid.astype(F32)).astype(BF16)
        lp = _dot(tri, lw_hi) + _dot(tri, lw_mid) + _dot(tri, lw_lo)
        lp_end = jnp.sum(jnp.where(last_row, lp, 0.0), axis=0, keepdims=True)
        k = chunk(k_ref, cc)
        v = chunk(v_ref, cc)
        b = chunk(b_ref, cc)
        e_neg = jnp.exp(-lp)
        e_end = jnp.exp(lp_end - lp)
        at = -(chunk(kk_ref, cc) * jnp.exp(lp - lw))
        rt = chunk(r_ref, cc) * jnp.exp(lp)
        bt = b * e_neg
        kt = k * e_neg
        bp = b * e_end
        kp = k * e_end
        p_end = jnp.exp(lp_end)
        for pr in pairs:
            sl = sls[pr]
            u = (cc, pr)
            q_p[u] = jnp.concatenate([at[:, sl], rt[:, sl]], axis=0)
            c_p[u] = jnp.concatenate([bt[:, sl], kt[:, sl]], axis=0).astype(BF16)
            d_t[u] = jnp.concatenate([bp[:, sl], kp[:, sl]], axis=0).T.astype(BF16)
            q_roll[u] = pltpu.roll(q_p[u], HEAD, axis=1)
            v_p[u] = v[:, sl]
            v_roll[u] = pltpu.roll(v_p[u], HEAD, axis=1)
            pe_p[u] = p_end[:, sl]
            pe_roll[u] = pltpu.roll(pe_p[u], HEAD, axis=1)

    units = [(cc, pr, hh) for cc in range(nch) for pr in pairs for hh in range(2)]
    a_top, a_bot, q_lo, v_hi, xw, npow = [], [], [], [], [], []
    zeros_c = jnp.zeros((2 * L, LANES), BF16)
    for cc, pr, hh in units:
        u = (cc, pr)
        if hh == 0:
            c_heads = jnp.concatenate([jnp.where(head_mask2[0], c_p[u], zeros_c),
                                       jnp.where(head_mask2[1], c_p[u], zeros_c)], axis=0)
            a_pair = _dot_nt(q_p[u].astype(BF16), c_heads)
        a_all = a_pair[:, hh * 2 * L:(hh + 1) * 2 * L]
        a_all = jnp.where(tri_mask, a_all, 0.0).astype(BF16)
        a_top.append(a_all[:L])
        a_bot.append(a_all[L:])
        q_lo.append(jnp.where(head_mask2[0], q_p[u] if hh == 0 else q_roll[u], 0.0))
        v_hi.append(jnp.where(head_mask1[1], v_p[u] if hh == 1 else v_roll[u], 0.0).astype(BF16))
    for i in range(len(units)):
        zv = jnp.concatenate([zeros_b, v_hi[i]], axis=0)
        xw.append(q_lo[i][:L] + _dot(a_top[i], zv))
        npow.append(a_top[i])
    for it in range(6):
        lo = {4: L // 4, 5: L // 2}.get(it, 0)
        for i in range(len(units)):
            xb = xw[i].astype(BF16)
            if it < 4:
                rhs = jnp.concatenate(
                    [jnp.concatenate([xb, npow[i]], axis=1),
                     jnp.zeros((L, 2 * LANES), BF16)], axis=0)
                res = _dot(npow[i], rhs)
                xw[i] = xw[i] + res[:, :LANES]
                npow[i] = res[:, LANES:].astype(BF16)
            elif it == 4:
                keep = L - lo
                rhs = jnp.concatenate(
                    [jnp.concatenate([xb[:keep], npow[i][:keep]], axis=1),
                     jnp.zeros((2 * L - keep, 2 * LANES), BF16)], axis=0)
                res = _dot(npow[i][lo:], rhs)
                xw[i] = jnp.concatenate([xw[i][:lo], xw[i][lo:] + res[:, :LANES]], axis=0)
                npow[i] = res[L // 2 - lo:, LANES:].astype(BF16)
            else:
                keep = L - lo
                rhs = jnp.concatenate([xb[:keep], jnp.zeros((2 * L - keep, LANES), BF16)], axis=0)
                res = _dot(npow[i], rhs)
                xw[i] = jnp.concatenate([xw[i][:lo], xw[i][lo:] + res], axis=0)
    lhs2 = {}
    for i, (cc, pr, hh) in enumerate(units):
        u = (cc, pr)
        m2 = jnp.concatenate([xw[i].astype(BF16), v_hi[i]], axis=0)
        both = _dot(jnp.concatenate([a_bot[i], d_t[u][hh * HEAD:(hh + 1) * HEAD, :]], axis=0), m2)
        ry = both[:L] + q_lo[i][L:]
        gh = both[L:]
        gh = gh + jnp.where(diag_mask, pe_p[u] if hh == 0 else pe_roll[u], 0.0)
        lhs2[(cc, pr, hh)] = jnp.concatenate([ry, gh], axis=0).astype(BF16)

    eye_rows = [jnp.where(head_mask1[hh], eye_pair, 0.0).astype(BF16) for hh in range(2)]
    n_pairs = N_HEADS // 2
    st = [st_ref[i] for i in range(nb * n_pairs)]
    for c2 in range(cpb):
        for g in range(nb):
            cc = g * cpb + c2
            for pr in pairs:
                si = g * n_pairs + pr
                rhs2 = jnp.concatenate(
                    [jnp.where(head_mask1[0], st[si], 0.0).astype(BF16), eye_rows[0],
                     jnp.where(head_mask1[1], st[si], 0.0).astype(BF16), eye_rows[1]], axis=0)
                out = _dot(jnp.concatenate([lhs2[(cc, pr, 0)], lhs2[(cc, pr, 1)]], axis=1), rhs2)
                y_ref[pl.ds(cc * L, L), sls[pr]] = out[:L]
                st[si] = out[L:]
    for i in range(nb * n_pairs):
        st_ref[i] = st[i]

    y = y_ref[...]
    half = D_RWKV // 2
    ones_half = ones_ref[0:half, 0:half]

    def head_sum(t):
        tb = t.astype(BF16)
        return jnp.concatenate([_dot(tb[:, :half], ones_half), _dot(tb[:, half:], ones_half)], axis=1)

    inv_n = 1.0 / HEAD
    mu = head_sum(y) * inv_n
    d = y - mu
    var = head_sum(d * d) * inv_n
    yn = d * lax.rsqrt(var + GN_EPS) * lng_ref[...] + lnb_ref[...]
    bonus = head_sum(whole(r_ref) * whole(k_ref) * rk_ref[...]) * whole(v_ref)
    out_all = ((yn + bonus) * whole(g_ref)).astype(BF16)
    for g in range(nb):
        o_ref[0, g, 0] = out_all[g * rc:(g + 1) * rc]


def _rwkv_rec(r, lw, k, v, kk, b, g, rk, lng, lnb, ones, batch, seq):
    rc = REC_CHUNKS * CHUNK
    nb = REC_BATCH
    shape5 = (batch // nb, nb, seq // rc, rc, D_RWKV)
    tok = pl.BlockSpec((1, nb, 1, rc, D_RWKV), lambda bi, ci: (bi, 0, ci, 0, 0))
    full = lambda a: pl.BlockSpec(a.shape, lambda bi, ci: (0,) * a.ndim)
    consts = (rk, lng, lnb, ones)
    out = pl.pallas_call(
        _rwkv_rec_body,
        grid=(batch // nb, seq // rc),
        in_specs=[tok] * 7 + [full(a) for a in consts],
        out_specs=tok,
        out_shape=jax.ShapeDtypeStruct(shape5, BF16),
        scratch_shapes=[pltpu.VMEM((nb * (N_HEADS // 2), HEAD, LANES), F32),
                        pltpu.VMEM((nb * rc, D_RWKV), F32)],
        compiler_params=pltpu.CompilerParams(
            dimension_semantics=("arbitrary", "arbitrary"), vmem_limit_bytes=VMEM_LIMIT),
        name="rwkv_rec",
    )(*[a.reshape(shape5) for a in (r, lw, k, v, kk, b, g)], *consts)
    return out.reshape(batch * seq, D_RWKV)


def _rms(x, g):
    ms = jnp.mean(x * x, axis=-1, keepdims=True)
    return x * lax.rsqrt(ms + RMS_EPS) * g


def _mix_out_body(x_ref, yr_ref, yc_ref, wo_ref, g2_ref, w1_ref, w2_ref, gf_ref, o_ref):
    y_cat = jnp.concatenate([yr_ref[...], yc_ref[...]], axis=1)
    x1 = x_ref[...] + _dot(y_cat, wo_ref[...])
    h = _rms(x1, g2_ref[...]).astype(BF16)
    acc = x1
    cols = D_FF // FF_SPLIT
    for c in range(FF_SPLIT):
        z = _dot(h, w1_ref[:, c * cols:(c + 1) * cols])
        z = jnp.square(jnp.maximum(z, 0.0)).astype(BF16)
        acc = acc + _dot(z, w2_ref[c * cols:(c + 1) * cols, :])
    o_ref[...] = _rms(acc, gf_ref[...])


def _mix_out(x2, yr, yc, wo, g2, w1, w2, gf):
    n = x2.shape[0]
    tm = TM_OUT
    tok = lambda w: pl.BlockSpec((tm, w), lambda i: (i, 0))
    full = lambda a: pl.BlockSpec(a.shape, lambda i: (0,) * a.ndim, pipeline_mode=pl.Buffered(1))
    consts = (wo, g2, w1, w2, gf)
    return pl.pallas_call(
        _mix_out_body,
        grid=(n // tm,),
        in_specs=[tok(D_MODEL), tok(D_RWKV), tok(D_CONV)] + [full(a) for a in consts],
        out_specs=tok(D_MODEL),
        out_shape=jax.ShapeDtypeStruct((n, D_MODEL), F32),
        compiler_params=pltpu.CompilerParams(
            dimension_semantics=("arbitrary",), vmem_limit_bytes=VMEM_LIMIT),
        name="mix_out",
    )(x2, yr, yc, *consts)


def kernel(x, norm_mix_g, w_in, shift_mu, w0, w_decay_up, a0, w_aaa_up, w_gate_up, k_k, k_a, r_k, ln_x_g, ln_x_b, conv_w, conv_b, conv_ln_g, conv_ln_b, w_out, norm_mlp_g, w_ff1, w_ff2, norm_final_g):
    batch, seq, _ = x.shape
    assert norm_mix_g.shape[0] == 1, "single layer"
    assert seq % TM_IN == 0 and seq % (REC_CHUNKS * CHUNK) == 0 and batch % REC_BATCH == 0 and (batch * seq) % TM_OUT == 0
    row = lambda a: a.reshape(1, -1)
    x2 = x.reshape(batch * seq, D_MODEL)

    zer = jnp.zeros((D_DECAY_LORA, D_RWKV), F32)
    wwa = jnp.concatenate(
        [jnp.concatenate([w_decay_up[0], zer], axis=1),
         jnp.concatenate([zer, w_aaa_up[0]], axis=1)], axis=0).astype(BF16)
    hid = jnp.arange(D_RWKV) // HEAD
    ones = (hid[:, None] == hid[None, :]).astype(BF16)

    r, lw, k, v, kk, b, g, yc = _mix_in(
        x2, row(norm_mix_g[0]), w_in[0].astype(BF16), row(shift_mu[0]), row(w0[0]), wwa,
        row(a0[0]), w_gate_up[0].astype(BF16), row(k_k[0]), row(k_a[0]), ones,
        jnp.broadcast_to(conv_w[0][:, None, :], (CONV_WIDTH, SUBLANES, D_CONV)),
        row(conv_b[0]), row(conv_ln_g[0]), row(conv_ln_b[0]), seq)
    yr = _rwkv_rec(r, lw, k, v, kk, b, g, row(r_k[0]), row(ln_x_g[0]), row(ln_x_b[0]), ones,
                   batch, seq)
    out = _mix_out(x2, yr, yc, w_out[0].astype(BF16), row(norm_mlp_g[0]),
                   w_ff1[0].astype(BF16), w_ff2[0].astype(BF16), row(norm_final_g))
    return out.reshape(batch, seq, D_MODEL)
```

```python
import functools

import jax
import jax.numpy as jnp
from jax import lax
from jax.experimental import pallas as pl
from jax.experimental.pallas import tpu as pltpu

F32 = jnp.float32
BF16 = jnp.bfloat16

D_MODEL = 1024
D_RWKV = 512
HEAD = 64
N_HEADS = D_RWKV // HEAD
D_CONV = 512
CONV_WIDTH = 31
D_DECAY_LORA = 64
D_AAA_LORA = 64
D_GATE_LORA = 128
D_FF = 4 * D_MODEL
D_RWKV_IN = 3 * D_RWKV + D_DECAY_LORA + D_AAA_LORA + D_GATE_LORA
D_IN = D_RWKV_IN + 2 * D_CONV
RMS_EPS = 1e-6
GN_EPS = 64e-5
LN_EPS = 1e-5
KK_NORM_FLOOR = 1e-12
DECAY_SCALE = 0.6065306597126334

LANES = 128
SUBLANES = 8
CHUNK = 64
REC_CHUNKS = 2
REC_BATCH = 4
CONV_HIST = 32
CONV_ROWS = 32
TM_IN = 512
TM_OUT = 1024
FF_SPLIT = 4
VMEM_LIMIT = 56 * 1024 * 1024


def _sigmoid(z):
    return 1.0 / (1.0 + jnp.exp(-z))


def _dot(a, b):
    return jnp.dot(a, b, preferred_element_type=F32)


def _dot_nt(a, b):
    return lax.dot_general(a, b, (((1,), (1,)), ((), ())), preferred_element_type=F32)


def _mix_in_body(tiles_per_seq,
                 x_ref, g_ref, win_ref, mu_ref, w0_ref, wwa_ref, a0_ref, wg_ref,
                 kkp_ref, ka_ref, ones_ref, cw_ref, cb_ref, clg_ref, clb_ref,
                 r_out, lw_out, k_out, v_out, kk_out, b_out, g_out, yc_out,
                 carry_ref, hbuf_ref):
    tm = x_ref.shape[0]

    @pl.when(pl.program_id(0) % tiles_per_seq == 0)
    def _():
        carry_ref[...] = jnp.zeros_like(carry_ref)
        hbuf_ref[0, 0:CONV_HIST, :] = jnp.zeros((CONV_HIST, D_CONV), F32)

    x = x_ref[...]
    ms = jnp.mean(x * x, axis=-1, keepdims=True)
    h = (x * lax.rsqrt(ms + RMS_EPS) * g_ref[...]).astype(BF16)
    p = _dot(h, win_ref[...])

    pr = p[:, :D_RWKV_IN]
    rolled = pltpu.roll(pr, 1, axis=0)
    row = lax.broadcasted_iota(jnp.int32, (SUBLANES, D_RWKV_IN), 0)
    first = jnp.where(row == 0, carry_ref[7:8, :], rolled[:SUBLANES])
    prev = jnp.concatenate([first, rolled[SUBLANES:]], axis=0)
    carry_ref[...] = pr[tm - 8:tm, :]
    pm = pr + (prev - pr) * mu_ref[...]

    r = pm[:, 0:D_RWKV]
    k = pm[:, D_RWKV:2 * D_RWKV]
    v = pm[:, 2 * D_RWKV:3 * D_RWKV]
    lora_in = pm[:, 3 * D_RWKV:3 * D_RWKV + LANES]
    g_down = pm[:, 3 * D_RWKV + LANES:D_RWKV_IN]

    lane = lax.broadcasted_iota(jnp.int32, lora_in.shape, 1)
    lora_act = jnp.where(lane < D_DECAY_LORA, jnp.tanh(lora_in), lora_in).astype(BF16)
    wa = _dot(lora_act, wwa_ref[...])
    lw_out[...] = -DECAY_SCALE * _sigmoid(w0_ref[...] + wa[:, :D_RWKV])
    a_sig = _sigmoid(a0_ref[...] + wa[:, D_RWKV:])
    g_out[...] = _dot(_sigmoid(g_down).astype(BF16), wg_ref[...])

    kk = k * kkp_ref[...]
    ss = _dot((kk * kk).astype(BF16), ones_ref[...])
    kk = kk * lax.rsqrt(jnp.maximum(ss, KK_NORM_FLOOR * KK_NORM_FLOOR))
    r_out[...] = r
    k_out[...] = k * (1.0 + (a_sig - 1.0) * ka_ref[...])
    v_out[...] = v
    kk_out[...] = kk
    b_out[...] = kk * a_sig

    u1 = p[:, D_RWKV_IN:D_RWKV_IN + D_CONV]
    u2 = p[:, D_RWKV_IN + D_CONV:D_IN]
    hbuf_ref[0, CONV_HIST:CONV_HIST + tm, :] = u1 * _sigmoid(u2)
    span = tm + CONV_HIST - SUBLANES
    for s in range(1, SUBLANES):
        hbuf_ref[s, 0:span, :] = hbuf_ref[0, pl.ds(s, span), :]
    base = CONV_HIST - (CONV_WIDTH - 1)
    for r0 in range(0, tm, CONV_ROWS):
        acc = jnp.broadcast_to(cb_ref[...], (CONV_ROWS, D_CONV))
        for j in range(CONV_WIDTH):
            off = base + j
            wj = jnp.concatenate([cw_ref[j]] * (CONV_ROWS // SUBLANES), axis=0)
            acc = acc + wj * hbuf_ref[off % SUBLANES, pl.ds(r0 + off - off % SUBLANES, CONV_ROWS), :]
        mu = jnp.mean(acc, axis=-1, keepdims=True)
        d = acc - mu
        var = jnp.mean(d * d, axis=-1, keepdims=True)
        hn = d * lax.rsqrt(var + LN_EPS) * clg_ref[...] + clb_ref[...]
        yc_out[r0:r0 + CONV_ROWS, :] = (hn * _sigmoid(hn)).astype(BF16)
    hbuf_ref[0, 0:CONV_HIST, :] = hbuf_ref[0, tm:tm + CONV_HIST, :]


def _mix_in(x2, g, win, mu, w0, wwa, a0, wg, kkp, ka, ones, cw, cb, clg, clb, seq):
    n = x2.shape[0]
    tm = TM_IN
    full = lambda a: pl.BlockSpec(a.shape, lambda i: (0,) * a.ndim, pipeline_mode=pl.Buffered(1))
    tok = lambda w: pl.BlockSpec((tm, w), lambda i: (i, 0))
    consts = (g, win, mu, w0, wwa, a0, wg, kkp, ka, ones, cw, cb, clg, clb)
    out_f32 = jax.ShapeDtypeStruct((n, D_RWKV), F32)
    return pl.pallas_call(
        functools.partial(_mix_in_body, seq // tm),
        grid=(n // tm,),
        in_specs=[tok(D_MODEL)] + [full(a) for a in consts],
        out_specs=[tok(D_RWKV)] * 7 + [tok(D_CONV)],
        out_shape=[out_f32] * 7 + [jax.ShapeDtypeStruct((n, D_CONV), BF16)],
        scratch_shapes=[pltpu.VMEM((SUBLANES, D_RWKV_IN), F32),
                        pltpu.VMEM((SUBLANES, CONV_HIST + tm, D_CONV), F32)],
        compiler_params=pltpu.CompilerParams(
            dimension_semantics=("arbitrary",), vmem_limit_bytes=VMEM_LIMIT),
        name="mix_in",
    )(x2, *consts)


def _rwkv_rec_body(r_ref, lw_ref, k_ref, v_ref, kk_ref, b_ref, g_ref,
                   rk_ref, lng_ref, lnb_ref, ones_ref,
                   o_ref, st_ref, y_ref):
    L = CHUNK
    nb, rc = r_ref.shape[1], r_ref.shape[3]
    cpb = rc // L
    nch = nb * cpb

    def chunk(ref, vc):
        return ref[0, vc // cpb, 0, pl.ds((vc % cpb) * L, L), :]

    def whole(ref):
        return jnp.concatenate([ref[0, g, 0] for g in range(nb)], axis=0)

    @pl.when(pl.program_id(1) == 0)
    def _():
        st_ref[...] = jnp.zeros_like(st_ref)

    ri = lax.broadcasted_iota(jnp.int32, (L, L), 0)
    ci = lax.broadcasted_iota(jnp.int32, (L, L), 1)
    tri = jnp.where(ri >= ci, 1.0, 0.0).astype(BF16)
    last_row = lax.broadcasted_iota(jnp.int32, (L, D_RWKV), 0) == L - 1
    lane = lax.broadcasted_iota(jnp.int32, (2 * L, LANES), 1)
    rowi = lax.broadcasted_iota(jnp.int32, (2 * L, LANES), 0)
    head_mask2 = (lane < HEAD, lane >= HEAD)
    t_idx = rowi & (L - 1)
    j_idx = lane & (L - 1)
    tri_mask = (j_idx < t_idx) | ((rowi >= L) & (j_idx == t_idx))
    lane1 = lax.broadcasted_iota(jnp.int32, (L, LANES), 1)
    row1 = lax.broadcasted_iota(jnp.int32, (L, LANES), 0)
    head_mask1 = (lane1 < HEAD, lane1 >= HEAD)
    diag_mask = row1 == lane1
    eye_pair = jnp.where(row1 == (lane1 & (L - 1)), 1.0, 0.0)
    zeros_b = jnp.zeros((L, LANES), BF16)
    pairs = range(N_HEADS // 2)
    sls = [slice(pr * LANES, (pr + 1) * LANES) for pr in pairs]

    q_p, c_p, d_t, q_roll, v_p, v_roll, pe_p, pe_roll = ({} for _ in range(8))
    for cc in range(nch):
        lw = chunk(lw_ref, cc)
        lw_hi = lw.astype(BF16)
        lw_lo = (lw - lw_hi.astype(F32)).astype(BF16)
        lp = _dot(tri, lw_hi) + _dot(tri, lw_lo)
        lp_end = jnp.sum(jnp.where(last_row, lp, 0.0), axis=0, keepdims=True)
        p_end = jnp.exp(lp_end)
        k = chunk(k_ref, cc)
        v = chunk(v_ref, cc)
        b = chunk(b_ref, cc)
        e_neg = jnp.exp(-lp)
        e_end = p_end * e_neg
        at = -(chunk(kk_ref, cc) * jnp.exp(lp - lw))
        rt = chunk(r_ref, cc) * jnp.exp(lp)
        bt = b * e_neg
        kt = k * e_neg
        bp = b * e_end
        kp = k * e_end
        for pr in pairs:
            sl = sls[pr]
            u = (cc, pr)
            q_p[u] = jnp.concatenate([at[:, sl], rt[:, sl]], axis=0)
            c_p[u] = jnp.concatenate([bt[:, sl], kt[:, sl]], axis=0).astype(BF16)
            d_t[u] = jnp.concatenate([bp[:, sl], kp[:, sl]], axis=0).T.astype(BF16)
            q_roll[u] = pltpu.roll(q_p[u], HEAD, axis=1)
            v_p[u] = v[:, sl]
            v_roll[u] = pltpu.roll(v_p[u], HEAD, axis=1)
            pe_p[u] = p_end[:, sl]
            pe_roll[u] = pltpu.roll(pe_p[u], HEAD, axis=1)

    units = [(cc, pr, hh) for cc in range(nch) for pr in pairs for hh in range(2)]
    a_top, a_bot, q_lo, v_hi, xw, npow = [], [], [], [], [], []
    zeros_c = jnp.zeros((2 * L, LANES), BF16)
    for cc, pr, hh in units:
        u = (cc, pr)
        if hh == 0:
            c_heads = jnp.concatenate([jnp.where(head_mask2[0], c_p[u], zeros_c),
                                       jnp.where(head_mask2[1], c_p[u], zeros_c)], axis=0)
            a_pair = _dot_nt(q_p[u].astype(BF16), c_heads)
        a_all = a_pair[:, hh * 2 * L:(hh + 1) * 2 * L]
        a_all = jnp.where(tri_mask, a_all, 0.0).astype(BF16)
        a_top.append(a_all[:L])
        a_bot.append(a_all[L:])
        q_lo.append(jnp.where(head_mask2[0], q_p[u] if hh == 0 else q_roll[u], 0.0))
        v_hi.append(jnp.where(head_mask1[1], v_p[u] if hh == 1 else v_roll[u], 0.0).astype(BF16))
    for i in range(len(units)):
        zv = jnp.concatenate([zeros_b, v_hi[i]], axis=0)
        xw.append(q_lo[i][:L] + _dot(a_top[i], zv))
        npow.append(a_top[i])
    for it in range(6):
        lo = {4: L // 4, 5: L // 2}.get(it, 0)
        for i in range(len(units)):
            xb = xw[i].astype(BF16)
            if it < 4:
                rhs = jnp.concatenate(
                    [jnp.concatenate([xb, npow[i]], axis=1),
                     jnp.zeros((L, 2 * LANES), BF16)], axis=0)
                res = _dot(npow[i], rhs)
                xw[i] = xw[i] + res[:, :LANES]
                npow[i] = res[:, LANES:].astype(BF16)
            elif it == 4:
                keep = L - lo
                rhs = jnp.concatenate(
                    [jnp.concatenate([xb[:keep], npow[i][:keep]], axis=1),
                     jnp.zeros((2 * L - keep, 2 * LANES), BF16)], axis=0)
                res = _dot(npow[i][lo:], rhs)
                xw[i] = jnp.concatenate([xw[i][:lo], xw[i][lo:] + res[:, :LANES]], axis=0)
                npow[i] = res[L // 2 - lo:, LANES:].astype(BF16)
            else:
                keep = L - lo
                rhs = jnp.concatenate([xb[:keep], jnp.zeros((2 * L - keep, LANES), BF16)], axis=0)
                res = _dot(npow[i], rhs)
                xw[i] = jnp.concatenate([xw[i][:lo], xw[i][lo:] + res], axis=0)
    lhs2 = {}
    for i, (cc, pr, hh) in enumerate(units):
        u = (cc, pr)
        m2 = jnp.concatenate([xw[i].astype(BF16), v_hi[i]], axis=0)
        both = _dot(jnp.concatenate([a_bot[i], d_t[u][hh * HEAD:(hh + 1) * HEAD, :]], axis=0), m2)
        ry = both[:L] + q_lo[i][L:]
        gh = both[L:]
        gh = gh + jnp.where(diag_mask, pe_p[u] if hh == 0 else pe_roll[u], 0.0)
        lhs2[(cc, pr, hh)] = jnp.concatenate([ry, gh], axis=0).astype(BF16)

    eye_rows = [jnp.where(head_mask1[hh], eye_pair, 0.0).astype(BF16) for hh in range(2)]
    n_pairs = N_HEADS // 2
    st = [st_ref[i] for i in range(nb * n_pairs)]
    for c2 in range(cpb):
        for g in range(nb):
            cc = g * cpb + c2
            for pr in pairs:
                si = g * n_pairs + pr
                rhs2 = jnp.concatenate(
                    [jnp.where(head_mask1[0], st[si], 0.0).astype(BF16), eye_rows[0],
                     jnp.where(head_mask1[1], st[si], 0.0).astype(BF16), eye_rows[1]], axis=0)
                out = _dot(jnp.concatenate([lhs2[(cc, pr, 0)], lhs2[(cc, pr, 1)]], axis=1), rhs2)
                y_ref[pl.ds(cc * L, L), sls[pr]] = out[:L]
                st[si] = out[L:]
    for i in range(nb * n_pairs):
        st_ref[i] = st[i]

    y = y_ref[...]
    half = D_RWKV // 2
    ones_half = ones_ref[0:half, 0:half]

    def head_sum(t):
        tb = t.astype(BF16)
        return jnp.concatenate([_dot(tb[:, :half], ones_half), _dot(tb[:, half:], ones_half)], axis=1)

    inv_n = 1.0 / HEAD
    mu = head_sum(y) * inv_n
    d = y - mu
    var = head_sum(d * d) * inv_n
    yn = d * lax.rsqrt(var + GN_EPS) * lng_ref[...] + lnb_ref[...]
    bonus = head_sum(whole(r_ref) * whole(k_ref) * rk_ref[...]) * whole(v_ref)
    out_all = ((yn + bonus) * whole(g_ref)).astype(BF16)
    for g in range(nb):
        o_ref[0, g, 0] = out_all[g * rc:(g + 1) * rc]


def _rwkv_rec(r, lw, k, v, kk, b, g, rk, lng, lnb, ones, batch, seq):
    rc = REC_CHUNKS * CHUNK
    nb = REC_BATCH
    shape5 = (batch // nb, nb, seq // rc, rc, D_RWKV)
    tok = pl.BlockSpec((1, nb, 1, rc, D_RWKV), lambda bi, ci: (bi, 0, ci, 0, 0))
    full = lambda a: pl.BlockSpec(a.shape, lambda bi, ci: (0,) * a.ndim)
    consts = (rk, lng, lnb, ones)
    out = pl.pallas_call(
        _rwkv_rec_body,
        grid=(batch // nb, seq // rc),
        in_specs=[tok] * 7 + [full(a) for a in consts],
        out_specs=tok,
        out_shape=jax.ShapeDtypeStruct(shape5, BF16),
        scratch_shapes=[pltpu.VMEM((nb * (N_HEADS // 2), HEAD, LANES), F32),
                        pltpu.VMEM((nb * rc, D_RWKV), F32)],
        compiler_params=pltpu.CompilerParams(
            dimension_semantics=("arbitrary", "arbitrary"), vmem_limit_bytes=VMEM_LIMIT),
        name="rwkv_rec",
    )(*[a.reshape(shape5) for a in (r, lw, k, v, kk, b, g)], *consts)
    return out.reshape(batch * seq, D_RWKV)


def _rms(x, g):
    ms = jnp.mean(x * x, axis=-1, keepdims=True)
    return x * lax.rsqrt(ms + RMS_EPS) * g


def _mix_out_body(x_ref, yr_ref, yc_ref, wo_ref, g2_ref, w1_ref, w2_ref, gf_ref, o_ref):
    y_cat = jnp.concatenate([yr_ref[...], yc_ref[...]], axis=1)
    x1 = x_ref[...] + _dot(y_cat, wo_ref[...])
    h = _rms(x1, g2_ref[...]).astype(BF16)
    acc = x1
    cols = D_FF // FF_SPLIT
    for c in range(FF_SPLIT):
        z = _dot(h, w1_ref[:, c * cols:(c + 1) * cols])
        z = jnp.square(jnp.maximum(z, 0.0)).astype(BF16)
        acc = acc + _dot(z, w2_ref[c * cols:(c + 1) * cols, :])
    o_ref[...] = _rms(acc, gf_ref[...])


def _mix_out(x2, yr, yc, wo, g2, w1, w2, gf):
    n = x2.shape[0]
    tm = TM_OUT
    tok = lambda w: pl.BlockSpec((tm, w), lambda i: (i, 0))
    full = lambda a: pl.BlockSpec(a.shape, lambda i: (0,) * a.ndim, pipeline_mode=pl.Buffered(1))
    consts = (wo, g2, w1, w2, gf)
    return pl.pallas_call(
        _mix_out_body,
        grid=(n // tm,),
        in_specs=[tok(D_MODEL), tok(D_RWKV), tok(D_CONV)] + [full(a) for a in consts],
        out_specs=tok(D_MODEL),
        out_shape=jax.ShapeDtypeStruct((n, D_MODEL), F32),
        compiler_params=pltpu.CompilerParams(
            dimension_semantics=("arbitrary",), vmem_limit_bytes=VMEM_LIMIT),
        name="mix_out",
    )(x2, yr, yc, *consts)


def kernel(x, norm_mix_g, w_in, shift_mu, w0, w_decay_up, a0, w_aaa_up, w_gate_up, k_k, k_a, r_k, ln_x_g, ln_x_b, conv_w, conv_b, conv_ln_g, conv_ln_b, w_out, norm_mlp_g, w_ff1, w_ff2, norm_final_g):
    batch, seq, _ = x.shape
    assert norm_mix_g.shape[0] == 1, "single layer"
    assert seq % TM_IN == 0 and seq % (REC_CHUNKS * CHUNK) == 0 and batch % REC_BATCH == 0 and (batch * seq) % TM_OUT == 0
    row = lambda a: a.reshape(1, -1)
    x2 = x.reshape(batch * seq, D_MODEL)

    zer = jnp.zeros((D_DECAY_LORA, D_RWKV), F32)
    wwa = jnp.concatenate(
        [jnp.concatenate([w_decay_up[0], zer], axis=1),
         jnp.concatenate([zer, w_aaa_up[0]], axis=1)], axis=0).astype(BF16)
    hid = jnp.arange(D_RWKV) // HEAD
    ones = (hid[:, None] == hid[None, :]).astype(BF16)

    r, lw, k, v, kk, b, g, yc = _mix_in(
        x2, row(norm_mix_g[0]), w_in[0].astype(BF16), row(shift_mu[0]), row(w0[0]), wwa,
        row(a0[0]), w_gate_up[0].astype(BF16), row(k_k[0]), row(k_a[0]), ones,
        jnp.broadcast_to(conv_w[0][:, None, :], (CONV_WIDTH, SUBLANES, D_CONV)),
        row(conv_b[0]), row(conv_ln_g[0]), row(conv_ln_b[0]), seq)
    yr = _rwkv_rec(r, lw, k, v, kk, b, g, row(r_k[0]), row(ln_x_g[0]), row(ln_x_b[0]), ones,
                   batch, seq)
    out = _mix_out(x2, yr, yc, w_out[0].astype(BF16), row(norm_mlp_g[0]),
                   w_ff1[0].astype(BF16), w_ff2[0].astype(BF16), row(norm_final_g))
    return out.reshape(batch, seq, D_MODEL)
```

```python
import functools

import jax
import jax.numpy as jnp
from jax import lax
from jax.experimental import pallas as pl
from jax.experimental.pallas import tpu as pltpu

F32 = jnp.float32
BF16 = jnp.bfloat16

D_MODEL = 1024
D_RWKV = 512
HEAD = 64
N_HEADS = D_RWKV // HEAD
D_CONV = 512
CONV_WIDTH = 31
D_DECAY_LORA = 64
D_AAA_LORA = 64
D_GATE_LORA = 128
D_FF = 4 * D_MODEL
D_RWKV_IN = 3 * D_RWKV + D_DECAY_LORA + D_AAA_LORA + D_GATE_LORA
D_IN = D_RWKV_IN + 2 * D_CONV
RMS_EPS = 1e-6
GN_EPS = 64e-5
LN_EPS = 1e-5
KK_NORM_FLOOR = 1e-12
DECAY_SCALE = 0.6065306597126334

LANES = 128
SUBLANES = 8
CHUNK = 64
REC_CHUNKS = 2
REC_BATCH = 4
CONV_HIST = 32
CONV_ROWS = 32
TM_IN = 512
TM_OUT = 1024
FF_SPLIT = 4
VMEM_LIMIT = 56 * 1024 * 1024


def _sigmoid(z):
    return 1.0 / (1.0 + jnp.exp(-z))


def _dot(a, b):
    return jnp.dot(a, b, preferred_element_type=F32)


def _dot_nt(a, b):
    return lax.dot_general(a, b, (((1,), (1,)), ((), ())), preferred_element_type=F32)


def _mix_in_body(tiles_per_seq,
                 x_ref, g_ref, win_ref, mu_ref, w0_ref, wwa_ref, a0_ref, wg_ref,
                 kkp_ref, ka_ref, ones_ref, cw_ref, cb_ref, clg_ref, clb_ref,
                 r_out, lw_out, k_out, v_out, kk_out, b_out, g_out, yc_out,
                 carry_ref, hbuf_ref):
    tm = x_ref.shape[0]

    @pl.when(pl.program_id(0) % tiles_per_seq == 0)
    def _():
        carry_ref[...] = jnp.zeros_like(carry_ref)
        hbuf_ref[0, 0:CONV_HIST, :] = jnp.zeros((CONV_HIST, D_CONV), F32)

    x = x_ref[...]
    ms = jnp.mean(x * x, axis=-1, keepdims=True)
    h = (x * lax.rsqrt(ms + RMS_EPS) * g_ref[...]).astype(BF16)
    p = _dot(h, win_ref[...])

    pr = p[:, :D_RWKV_IN]
    rolled = pltpu.roll(pr, 1, axis=0)
    row = lax.broadcasted_iota(jnp.int32, (SUBLANES, D_RWKV_IN), 0)
    last = carry_ref[SUBLANES - 1:SUBLANES, :]
    first = jnp.where(row == 0, last, rolled[:SUBLANES])
    prev = jnp.concatenate([first, rolled[SUBLANES:]], axis=0)
    carry_ref[...] = pr[tm - SUBLANES:tm, :]
    pm = pr + (prev - pr) * mu_ref[...]

    r = pm[:, 0:D_RWKV]
    k = pm[:, D_RWKV:2 * D_RWKV]
    v = pm[:, 2 * D_RWKV:3 * D_RWKV]
    lora_in = pm[:, 3 * D_RWKV:3 * D_RWKV + LANES]
    g_down = pm[:, 3 * D_RWKV + LANES:D_RWKV_IN]

    lane = lax.broadcasted_iota(jnp.int32, lora_in.shape, 1)
    lora_act = jnp.where(lane < D_DECAY_LORA, jnp.tanh(lora_in), lora_in).astype(BF16)
    wa = _dot(lora_act, wwa_ref[...])
    lw_out[...] = -DECAY_SCALE * _sigmoid(w0_ref[...] + wa[:, :D_RWKV])
    a_sig = _sigmoid(a0_ref[...] + wa[:, D_RWKV:])
    g_out[...] = _dot(_sigmoid(g_down).astype(BF16), wg_ref[...])

    kk = k * kkp_ref[...]
    ss = _dot((kk * kk).astype(BF16), ones_ref[...])
    kk = kk * lax.rsqrt(jnp.maximum(ss, KK_NORM_FLOOR * KK_NORM_FLOOR))
    r_out[...] = r
    ka = ka_ref[...]
    k_out[...] = k * (a_sig * ka + (1.0 - ka))
    v_out[...] = v
    kk_out[...] = kk
    b_out[...] = kk * a_sig

    u1 = p[:, D_RWKV_IN:D_RWKV_IN + D_CONV]
    u2 = p[:, D_RWKV_IN + D_CONV:D_IN]
    hbuf_ref[0, CONV_HIST:CONV_HIST + tm, :] = u1 * _sigmoid(u2)
    span = tm + CONV_HIST - SUBLANES
    for s in range(1, SUBLANES):
        hbuf_ref[s, 0:span, :] = hbuf_ref[0, pl.ds(s, span), :]
    base = CONV_HIST - (CONV_WIDTH - 1)
    for r0 in range(0, tm, CONV_ROWS):
        acc = jnp.broadcast_to(cb_ref[...], (CONV_ROWS, D_CONV))
        for j in range(CONV_WIDTH):
            off = base + j
            wj = jnp.concatenate([cw_ref[j]] * (CONV_ROWS // SUBLANES), axis=0)
            acc = acc + wj * hbuf_ref[off % SUBLANES, pl.ds(r0 + off - off % SUBLANES, CONV_ROWS), :]
        mu = jnp.mean(acc, axis=-1, keepdims=True)
        d = acc - mu
        var = jnp.mean(d * d, axis=-1, keepdims=True)
        hn = d * lax.rsqrt(var + LN_EPS) * clg_ref[...] + clb_ref[...]
        yc_out[r0:r0 + CONV_ROWS, :] = (hn * _sigmoid(hn)).astype(BF16)
    hbuf_ref[0, 0:CONV_HIST, :] = hbuf_ref[0, tm:tm + CONV_HIST, :]


def _mix_in(x2, g, win, mu, w0, wwa, a0, wg, kkp, ka, ones, cw, cb, clg, clb, seq):
    n = x2.shape[0]
    tm = TM_IN
    full = lambda a: pl.BlockSpec(a.shape, lambda i: (0,) * a.ndim, pipeline_mode=pl.Buffered(1))
    tok = lambda w: pl.BlockSpec((tm, w), lambda i: (i, 0))
    consts = (g, win, mu, w0, wwa, a0, wg, kkp, ka, ones, cw, cb, clg, clb)
    out_f32 = jax.ShapeDtypeStruct((n, D_RWKV), F32)
    return pl.pallas_call(
        functools.partial(_mix_in_body, seq // tm),
        grid=(n // tm,),
        in_specs=[tok(D_MODEL)] + [full(a) for a in consts],
        out_specs=[tok(D_RWKV)] * 7 + [tok(D_CONV)],
        out_shape=[out_f32] * 7 + [jax.ShapeDtypeStruct((n, D_CONV), BF16)],
        scratch_shapes=[pltpu.VMEM((SUBLANES, D_RWKV_IN), F32),
                        pltpu.VMEM((SUBLANES, CONV_HIST + tm, D_CONV), F32)],
        compiler_params=pltpu.CompilerParams(
            dimension_semantics=("arbitrary",), vmem_limit_bytes=VMEM_LIMIT),
        name="mix_in",
    )(x2, *consts)


def _rwkv_rec_body(r_ref, lw_ref, k_ref, v_ref, kk_ref, b_ref, g_ref,
                   rk_ref, lng_ref, lnb_ref, ones_ref,
                   o_ref, st_ref, y_ref):
    L = CHUNK
    nb, rc = r_ref.shape[1], r_ref.shape[3]
    cpb = rc // L
    nch = nb * cpb

    def chunk(ref, vc):
        return ref[0, vc // cpb, 0, pl.ds((vc % cpb) * L, L), :]

    def whole(ref):
        return jnp.concatenate([ref[0, g, 0] for g in range(nb)], axis=0)

    @pl.when(pl.program_id(1) == 0)
    def _():
        st_ref[...] = jnp.zeros_like(st_ref)

    ri = lax.broadcasted_iota(jnp.int32, (L, L), 0)
    ci = lax.broadcasted_iota(jnp.int32, (L, L), 1)
    tri = jnp.where(ri >= ci, 1.0, 0.0).astype(BF16)
    last_row = lax.broadcasted_iota(jnp.int32, (L, D_RWKV), 0) == L - 1
    lane = lax.broadcasted_iota(jnp.int32, (2 * L, LANES), 1)
    rowi = lax.broadcasted_iota(jnp.int32, (2 * L, LANES), 0)
    head_mask2 = (lane < HEAD, lane >= HEAD)
    t_idx = rowi & (L - 1)
    j_idx = lane & (L - 1)
    tri_mask = (j_idx < t_idx) | ((rowi >= L) & (j_idx == t_idx))
    lane1 = lax.broadcasted_iota(jnp.int32, (L, LANES), 1)
    row1 = lax.broadcasted_iota(jnp.int32, (L, LANES), 0)
    head_mask1 = (lane1 < HEAD, lane1 >= HEAD)
    diag_mask = row1 == lane1
    eye_pair = jnp.where(row1 == (lane1 & (L - 1)), 1.0, 0.0)
    zeros_b = jnp.zeros((L, LANES), BF16)
    pairs = range(N_HEADS // 2)
    sls = [slice(pr * LANES, (pr + 1) * LANES) for pr in pairs]

    q_p, c_p, d_t, q_roll, v_p, v_roll, pe_p, pe_roll = ({} for _ in range(8))
    for cc in range(nch):
        lw = chunk(lw_ref, cc)
        lw_hi = lw.astype(BF16)
        lw_lo = (lw - lw_hi.astype(F32)).astype(BF16)
        lp = _dot(tri, lw_hi) + _dot(tri, lw_lo)
        lp_end = jnp.sum(jnp.where(last_row, lp, 0.0), axis=0, keepdims=True)
        p_end = jnp.exp(lp_end)
        k = chunk(k_ref, cc)
        v = chunk(v_ref, cc)
        b = chunk(b_ref, cc)
        e_neg = jnp.exp(-lp)
        e_end = p_end * e_neg
        at = -(chunk(kk_ref, cc) * jnp.exp(lp - lw))
        rt = chunk(r_ref, cc) * jnp.exp(lp)
        bt = b * e_neg
        kt = k * e_neg
        bp = b * e_end
        kp = k * e_end
        for pr in pairs:
            sl = sls[pr]
            u = (cc, pr)
            q_p[u] = jnp.concatenate([at[:, sl], rt[:, sl]], axis=0)
            c_p[u] = jnp.concatenate([bt[:, sl], kt[:, sl]], axis=0).astype(BF16)
            d_t[u] = jnp.concatenate([bp[:, sl], kp[:, sl]], axis=0).T.astype(BF16)
            q_roll[u] = pltpu.roll(q_p[u], HEAD, axis=1)
            v_p[u] = v[:, sl]
            v_roll[u] = pltpu.roll(v_p[u], HEAD, axis=1)
            pe_p[u] = p_end[:, sl]
            pe_roll[u] = pltpu.roll(pe_p[u], HEAD, axis=1)

    units = [(cc, pr, hh) for cc in range(nch) for pr in pairs for hh in range(2)]
    a_top, a_bot, q_lo, v_hi, xw, npow = [], [], [], [], [], []
    zeros_c = jnp.zeros((2 * L, LANES), BF16)
    for cc, pr, hh in units:
        u = (cc, pr)
        if hh == 0:
            c_heads = jnp.concatenate([jnp.where(head_mask2[0], c_p[u], zeros_c),
                                       jnp.where(head_mask2[1], c_p[u], zeros_c)], axis=0)
            a_pair = _dot_nt(q_p[u].astype(BF16), c_heads)
        a_all = a_pair[:, hh * 2 * L:(hh + 1) * 2 * L]
        a_all = jnp.where(tri_mask, a_all, 0.0).astype(BF16)
        a_top.append(a_all[:L])
        a_bot.append(a_all[L:])
        q_lo.append(jnp.where(head_mask2[0], q_p[u] if hh == 0 else q_roll[u], 0.0))
        v_hi.append(jnp.where(head_mask1[1], v_p[u] if hh == 1 else v_roll[u], 0.0).astype(BF16))
    for i in range(len(units)):
        zv = jnp.concatenate([zeros_b, v_hi[i]], axis=0)
        xw.append(q_lo[i][:L] + _dot(a_top[i], zv))
        npow.append(a_top[i])
    for it in range(6):
        lo = {4: L // 4, 5: L // 2}.get(it, 0)
        for i in range(len(units)):
            xb = xw[i].astype(BF16)
            if it < 4:
                rhs = jnp.concatenate(
                    [jnp.concatenate([xb, npow[i]], axis=1),
                     jnp.zeros((L, 2 * LANES), BF16)], axis=0)
                res = _dot(npow[i], rhs)
                xw[i] = xw[i] + res[:, :LANES]
                npow[i] = res[:, LANES:].astype(BF16)
            elif it == 4:
                keep = L - lo
                rhs = jnp.concatenate(
                    [jnp.concatenate([xb[:keep], npow[i][:keep]], axis=1),
                     jnp.zeros((2 * L - keep, 2 * LANES), BF16)], axis=0)
                res = _dot(npow[i][lo:], rhs)
                xw[i] = jnp.concatenate([xw[i][:lo], xw[i][lo:] + res[:, :LANES]], axis=0)
                npow[i] = res[L // 2 - lo:, LANES:].astype(BF16)
            else:
                keep = L - lo
                rhs = jnp.concatenate([xb[:keep], jnp.zeros((2 * L - keep, LANES), BF16)], axis=0)
                res = _dot(npow[i], rhs)
                xw[i] = jnp.concatenate([xw[i][:lo], xw[i][lo:] + res], axis=0)
    lhs2 = {}
    for i, (cc, pr, hh) in enumerate(units):
        u = (cc, pr)
        m2 = jnp.concatenate([xw[i].astype(BF16), v_hi[i]], axis=0)
        both = _dot(jnp.concatenate([a_bot[i], d_t[u][hh * HEAD:(hh + 1) * HEAD, :]], axis=0), m2)
        ry = both[:L] + q_lo[i][L:]
        gh = both[L:]
        gh = gh + jnp.where(diag_mask, pe_p[u] if hh == 0 else pe_roll[u], 0.0)
        lhs2[(cc, pr, hh)] = jnp.concatenate([ry, gh], axis=0).astype(BF16)

    eye_rows = [jnp.where(head_mask1[hh], eye_pair, 0.0).astype(BF16) for hh in range(2)]
    n_pairs = N_HEADS // 2
    st = [st_ref[i] for i in range(nb * n_pairs)]
    for c2 in range(cpb):
        for g in range(nb):
            cc = g * cpb + c2
            for pr in pairs:
                si = g * n_pairs + pr
                rhs2 = jnp.concatenate(
                    [jnp.where(head_mask1[0], st[si], 0.0).astype(BF16), eye_rows[0],
                     jnp.where(head_mask1[1], st[si], 0.0).astype(BF16), eye_rows[1]], axis=0)
                out = _dot(jnp.concatenate([lhs2[(cc, pr, 0)], lhs2[(cc, pr, 1)]], axis=1), rhs2)
                y_ref[pl.ds(cc * L, L), sls[pr]] = out[:L]
                st[si] = out[L:]
    for i in range(nb * n_pairs):
        st_ref[i] = st[i]

    y = y_ref[...]
    half = D_RWKV // 2
    ones_half = ones_ref[0:half, 0:half]

    def head_sum(t):
        tb = t.astype(BF16)
        return jnp.concatenate([_dot(tb[:, :half], ones_half), _dot(tb[:, half:], ones_half)], axis=1)

    inv_n = 1.0 / HEAD
    mu = head_sum(y) * inv_n
    d = y - mu
    var = head_sum(d * d) * inv_n
    yn = d * lax.rsqrt(var + GN_EPS) * lng_ref[...] + lnb_ref[...]
    bonus = head_sum(whole(r_ref) * whole(k_ref) * rk_ref[...]) * whole(v_ref)
    out_all = ((yn + bonus) * whole(g_ref)).astype(BF16)
    for g in range(nb):
        o_ref[0, g, 0] = out_all[g * rc:(g + 1) * rc]


def _rwkv_rec(r, lw, k, v, kk, b, g, rk, lng, lnb, ones, batch, seq):
    rc = REC_CHUNKS * CHUNK
    nb = REC_BATCH
    shape5 = (batch // nb, nb, seq // rc, rc, D_RWKV)
    tok = pl.BlockSpec((1, nb, 1, rc, D_RWKV), lambda bi, ci: (bi, 0, ci, 0, 0))
    full = lambda a: pl.BlockSpec(a.shape, lambda bi, ci: (0,) * a.ndim)
    consts = (rk, lng, lnb, ones)
    out = pl.pallas_call(
        _rwkv_rec_body,
        grid=(batch // nb, seq // rc),
        in_specs=[tok] * 7 + [full(a) for a in consts],
        out_specs=tok,
        out_shape=jax.ShapeDtypeStruct(shape5, BF16),
        scratch_shapes=[pltpu.VMEM((nb * (N_HEADS // 2), HEAD, LANES), F32),
                        pltpu.VMEM((nb * rc, D_RWKV), F32)],
        compiler_params=pltpu.CompilerParams(
            dimension_semantics=("arbitrary", "arbitrary"), vmem_limit_bytes=VMEM_LIMIT),
        name="rwkv_rec",
    )(*[a.reshape(shape5) for a in (r, lw, k, v, kk, b, g)], *consts)
    return out.reshape(batch * seq, D_RWKV)


def _rms(x, g):
    ms = jnp.mean(x * x, axis=-1, keepdims=True)
    return x * lax.rsqrt(ms + RMS_EPS) * g


def _mix_out_body(x_ref, yr_ref, yc_ref, wo_ref, g2_ref, w1_ref, w2_ref, gf_ref, o_ref):
    y_cat = jnp.concatenate([yr_ref[...], yc_ref[...]], axis=1)
    x1 = x_ref[...] + _dot(y_cat, wo_ref[...])
    h = _rms(x1, g2_ref[...]).astype(BF16)
    acc = x1
    cols = D_FF // FF_SPLIT
    for c in range(FF_SPLIT):
        z = _dot(h, w1_ref[:, c * cols:(c + 1) * cols])
        z = jnp.square(jnp.maximum(z, 0.0)).astype(BF16)
        acc = acc + _dot(z, w2_ref[c * cols:(c + 1) * cols, :])
    o_ref[...] = _rms(acc, gf_ref[...])


def _mix_out(x2, yr, yc, wo, g2, w1, w2, gf):
    n = x2.shape[0]
    tm = TM_OUT
    tok = lambda w: pl.BlockSpec((tm, w), lambda i: (i, 0))
    full = lambda a: pl.BlockSpec(a.shape, lambda i: (0,) * a.ndim, pipeline_mode=pl.Buffered(1))
    consts = (wo, g2, w1, w2, gf)
    return pl.pallas_call(
        _mix_out_body,
        grid=(n // tm,),
        in_specs=[tok(D_MODEL), tok(D_RWKV), tok(D_CONV)] + [full(a) for a in consts],
        out_specs=tok(D_MODEL),
        out_shape=jax.ShapeDtypeStruct((n, D_MODEL), F32),
        compiler_params=pltpu.CompilerParams(
            dimension_semantics=("arbitrary",), vmem_limit_bytes=VMEM_LIMIT),
        name="mix_out",
    )(x2, yr, yc, *consts)


def kernel(x, norm_mix_g, w_in, shift_mu, w0, w_decay_up, a0, w_aaa_up, w_gate_up, k_k, k_a, r_k, ln_x_g, ln_x_b, conv_w, conv_b, conv_ln_g, conv_ln_b, w_out, norm_mlp_g, w_ff1, w_ff2, norm_final_g):
    batch, seq, _ = x.shape
    assert norm_mix_g.shape[0] == 1, "single layer"
    assert seq % TM_IN == 0 and seq % (REC_CHUNKS * CHUNK) == 0 and batch % REC_BATCH == 0 and (batch * seq) % TM_OUT == 0
    row = lambda a: a.reshape(1, -1)
    x2 = x.reshape(batch * seq, D_MODEL)

    zer = jnp.zeros((D_DECAY_LORA, D_RWKV), F32)
    wwa = jnp.concatenate(
        [jnp.concatenate([w_decay_up[0], zer], axis=1),
         jnp.concatenate([zer, w_aaa_up[0]], axis=1)], axis=0).astype(BF16)
    hid = jnp.arange(D_RWKV) // HEAD
    ones = (hid[:, None] == hid[None, :]).astype(BF16)

    r, lw, k, v, kk, b, g, yc = _mix_in(
        x2, row(norm_mix_g[0]), w_in[0].astype(BF16), row(shift_mu[0]), row(w0[0]), wwa,
        row(a0[0]), w_gate_up[0].astype(BF16), row(k_k[0]), row(k_a[0]), ones,
        jnp.broadcast_to(conv_w[0][:, None, :], (CONV_WIDTH, SUBLANES, D_CONV)),
        row(conv_b[0]), row(conv_ln_g[0]), row(conv_ln_b[0]), seq)
    yr = _rwkv_rec(r, lw, k, v, kk, b, g, row(r_k[0]), row(ln_x_g[0]), row(ln_x_b[0]), ones,
                   batch, seq)
    out = _mix_out(x2, yr, yc, w_out[0].astype(BF16), row(norm_mlp_g[0]),
                   w_ff1[0].astype(BF16), w_ff2[0].astype(BF16), row(norm_final_g))
    return out.reshape(batch, seq, D_MODEL)
```

```python
import functools

import jax
import jax.numpy as jnp
from jax import lax
from jax.experimental import pallas as pl
from jax.experimental.pallas import tpu as pltpu

F32 = jnp.float32
BF16 = jnp.bfloat16

D_MODEL = 1024
D_RWKV = 512
HEAD = 64
N_HEADS = D_RWKV // HEAD
D_CONV = 512
CONV_WIDTH = 31
D_DECAY_LORA = 64
D_AAA_LORA = 64
D_GATE_LORA = 128
D_FF = 4 * D_MODEL
D_RWKV_IN = 3 * D_RWKV + D_DECAY_LORA + D_AAA_LORA + D_GATE_LORA
D_IN = D_RWKV_IN + 2 * D_CONV
RMS_EPS = 1e-6
GN_EPS = 64e-5
LN_EPS = 1e-5
KK_NORM_FLOOR = 1e-12
DECAY_SCALE = 0.6065306597126334
LOG2_E = 1.4426950408889634

LANES = 128
SUBLANES = 8
CHUNK = 64
REC_CHUNKS = 2
REC_BATCH = 4
CONV_HIST = 32
CONV_ROWS = 32
TM_IN = 512
TM_OUT = 1024
FF_SPLIT = 4
VMEM_LIMIT = 56 * 1024 * 1024


def _sigmoid(z):
    return 1.0 / (1.0 + jnp.exp2(z * -LOG2_E))


def _dot(a, b):
    return jnp.dot(a, b, preferred_element_type=F32)


def _dot_nt(a, b):
    return lax.dot_general(a, b, (((1,), (1,)), ((), ())), preferred_element_type=F32)


def _mix_in_body(tiles_per_seq,
                 x_ref, g_ref, win_ref, mu_ref, w0_ref, wwa_ref, a0_ref, wg_ref,
                 kkp_ref, ka_ref, ones_ref, cw_ref, cb_ref, clg_ref, clb_ref,
                 r_out, lw_out, k_out, v_out, kk_out, b_out, g_out, yc_out,
                 carry_ref, hbuf_ref):
    tm = x_ref.shape[0]

    @pl.when(pl.program_id(0) % tiles_per_seq == 0)
    def _():
        carry_ref[...] = jnp.zeros_like(carry_ref)
        hbuf_ref[0, 0:CONV_HIST, :] = jnp.zeros((CONV_HIST, D_CONV), F32)

    x = x_ref[...]
    ms = jnp.mean(x * x, axis=-1, keepdims=True)
    h = (x * lax.rsqrt(ms + RMS_EPS) * g_ref[...]).astype(BF16)
    p = _dot(h, win_ref[...])

    pr = p[:, :D_RWKV_IN]
    rolled = pltpu.roll(pr, 1, axis=0)
    row = lax.broadcasted_iota(jnp.int32, (SUBLANES, D_RWKV_IN), 0)
    last = carry_ref[SUBLANES - 1:SUBLANES, :]
    first = jnp.where(row == 0, last, rolled[:SUBLANES])
    prev = jnp.concatenate([first, rolled[SUBLANES:]], axis=0)
    carry_ref[...] = pr[tm - SUBLANES:tm, :]
    pm = pr + (prev - pr) * mu_ref[...]

    r = pm[:, 0:D_RWKV]
    k = pm[:, D_RWKV:2 * D_RWKV]
    v = pm[:, 2 * D_RWKV:3 * D_RWKV]
    lora_in = pm[:, 3 * D_RWKV:3 * D_RWKV + LANES]
    g_down = pm[:, 3 * D_RWKV + LANES:D_RWKV_IN]

    lane = lax.broadcasted_iota(jnp.int32, lora_in.shape, 1)
    lora_act = jnp.where(lane < D_DECAY_LORA, jnp.tanh(lora_in), lora_in).astype(BF16)
    wa = _dot(lora_act, wwa_ref[...])
    lw_out[...] = -DECAY_SCALE * _sigmoid(w0_ref[...] + wa[:, :D_RWKV])
    a_sig = _sigmoid(a0_ref[...] + wa[:, D_RWKV:])
    g_out[...] = _dot(_sigmoid(g_down).astype(BF16), wg_ref[...])

    kk = k * kkp_ref[...]
    ss = _dot((kk * kk).astype(BF16), ones_ref[...])
    kk = kk * lax.rsqrt(jnp.maximum(ss, KK_NORM_FLOOR * KK_NORM_FLOOR))
    r_out[...] = r
    ka = ka_ref[...]
    k_out[...] = k * (a_sig * ka + (1.0 - ka))
    v_out[...] = v
    kk_out[...] = kk
    b_out[...] = kk * a_sig

    u1 = p[:, D_RWKV_IN:D_RWKV_IN + D_CONV]
    u2 = p[:, D_RWKV_IN + D_CONV:D_IN]
    hbuf_ref[0, CONV_HIST:CONV_HIST + tm, :] = u1 * _sigmoid(u2)
    span = tm + CONV_HIST - SUBLANES
    groups = (CONV_HIST + tm) // SUBLANES
    glu_rows = hbuf_ref[0].reshape(groups, SUBLANES, D_CONV)
    sub = lax.broadcasted_iota(jnp.int32, (groups - 1, SUBLANES, D_CONV), 1)
    for s in range(1, SUBLANES):
        mixed = jnp.where(sub >= s, glu_rows[:-1], glu_rows[1:])
        hbuf_ref[s, 0:span, :] = pltpu.roll(mixed, SUBLANES - s, axis=1).reshape(span, D_CONV)
    base = CONV_HIST - (CONV_WIDTH - 1)
    for r0 in range(0, tm, CONV_ROWS):
        acc = jnp.broadcast_to(cb_ref[...], (CONV_ROWS, D_CONV))
        for j in range(CONV_WIDTH):
            off = base + j
            wj = jnp.concatenate([cw_ref[j]] * (CONV_ROWS // SUBLANES), axis=0)
            acc = acc + wj * hbuf_ref[off % SUBLANES, pl.ds(r0 + off - off % SUBLANES, CONV_ROWS), :]
        mu = jnp.mean(acc, axis=-1, keepdims=True)
        d = acc - mu
        var = jnp.mean(d * d, axis=-1, keepdims=True)
        hn = d * lax.rsqrt(var + LN_EPS) * clg_ref[...] + clb_ref[...]
        yc_out[r0:r0 + CONV_ROWS, :] = (hn * _sigmoid(hn)).astype(BF16)
    hbuf_ref[0, 0:CONV_HIST, :] = hbuf_ref[0, tm:tm + CONV_HIST, :]


def _mix_in(x2, g, win, mu, w0, wwa, a0, wg, kkp, ka, ones, cw, cb, clg, clb, seq):
    n = x2.shape[0]
    tm = TM_IN
    full = lambda a: pl.BlockSpec(a.shape, lambda i: (0,) * a.ndim, pipeline_mode=pl.Buffered(1))
    tok = lambda w: pl.BlockSpec((tm, w), lambda i: (i, 0))
    consts = (g, win, mu, w0, wwa, a0, wg, kkp, ka, ones, cw, cb, clg, clb)
    out_f32 = jax.ShapeDtypeStruct((n, D_RWKV), F32)
    return pl.pallas_call(
        functools.partial(_mix_in_body, seq // tm),
        grid=(n // tm,),
        in_specs=[tok(D_MODEL)] + [full(a) for a in consts],
        out_specs=[tok(D_RWKV)] * 7 + [tok(D_CONV)],
        out_shape=[out_f32] * 7 + [jax.ShapeDtypeStruct((n, D_CONV), BF16)],
        scratch_shapes=[pltpu.VMEM((SUBLANES, D_RWKV_IN), F32),
                        pltpu.VMEM((SUBLANES, CONV_HIST + tm, D_CONV), F32)],
        compiler_params=pltpu.CompilerParams(
            dimension_semantics=("arbitrary",), vmem_limit_bytes=VMEM_LIMIT),
        name="mix_in",
    )(x2, *consts)


def _rwkv_rec_body(r_ref, lw_ref, k_ref, v_ref, kk_ref, b_ref, g_ref,
                   rk_ref, lng_ref, lnb_ref, ones_ref,
                   o_ref, st_ref, y_ref):
    L = CHUNK
    nb, rc = r_ref.shape[1], r_ref.shape[3]
    cpb = rc // L
    nch = nb * cpb

    def chunk(ref, vc):
        return ref[0, vc // cpb, 0, pl.ds((vc % cpb) * L, L), :]

    def whole(ref):
        return jnp.concatenate([ref[0, g, 0] for g in range(nb)], axis=0)

    @pl.when(pl.program_id(1) == 0)
    def _():
        st_ref[...] = jnp.zeros_like(st_ref)

    ri = lax.broadcasted_iota(jnp.int32, (L, L), 0)
    ci = lax.broadcasted_iota(jnp.int32, (L, L), 1)
    tri = jnp.where(ri >= ci, 1.0, 0.0).astype(BF16)
    last_row = lax.broadcasted_iota(jnp.int32, (L, D_RWKV), 0) == L - 1
    lane = lax.broadcasted_iota(jnp.int32, (2 * L, LANES), 1)
    rowi = lax.broadcasted_iota(jnp.int32, (2 * L, LANES), 0)
    head_mask2 = (lane < HEAD, lane >= HEAD)
    t_idx = rowi & (L - 1)
    j_idx = lane & (L - 1)
    tri_mask = (j_idx < t_idx) | ((rowi >= L) & (j_idx == t_idx))
    lane1 = lax.broadcasted_iota(jnp.int32, (L, LANES), 1)
    row1 = lax.broadcasted_iota(jnp.int32, (L, LANES), 0)
    head_mask1 = (lane1 < HEAD, lane1 >= HEAD)
    diag_mask = row1 == lane1
    eye_pair = jnp.where(row1 == (lane1 & (L - 1)), 1.0, 0.0)
    zeros_b = jnp.zeros((L, LANES), BF16)
    pairs = range(N_HEADS // 2)
    sls = [slice(pr * LANES, (pr + 1) * LANES) for pr in pairs]

    q_p, c_p, d_t, q_roll, v_p, v_roll, pe_p, pe_roll = ({} for _ in range(8))
    for cc in range(nch):
        lw = chunk(lw_ref, cc)
        lw_hi = lw.astype(BF16)
        lw_lo = (lw - lw_hi.astype(F32)).astype(BF16)
        lp = _dot(tri, lw_hi) + _dot(tri, lw_lo)
        lp_end = jnp.sum(jnp.where(last_row, lp, 0.0), axis=0, keepdims=True)
        p_end = jnp.exp(lp_end)
        k = chunk(k_ref, cc)
        v = chunk(v_ref, cc)
        b = chunk(b_ref, cc)
        e_neg = jnp.exp2(lp * -LOG2_E)
        e_end = p_end * e_neg
        at = -(chunk(kk_ref, cc) * jnp.exp(lp - lw))
        rt = chunk(r_ref, cc) * jnp.exp(lp)
        bt = b * e_neg
        kt = k * e_neg
        bp = b * e_end
        kp = k * e_end
        for pr in pairs:
            sl = sls[pr]
            u = (cc, pr)
            q_p[u] = jnp.concatenate([at[:, sl], rt[:, sl]], axis=0)
            c_p[u] = jnp.concatenate([bt[:, sl], kt[:, sl]], axis=0).astype(BF16)
            d_t[u] = jnp.concatenate([bp[:, sl], kp[:, sl]], axis=0).T.astype(BF16)
            q_roll[u] = pltpu.roll(q_p[u], HEAD, axis=1)
            v_p[u] = v[:, sl]
            v_roll[u] = pltpu.roll(v_p[u], HEAD, axis=1)
            pe_p[u] = p_end[:, sl]
            pe_roll[u] = pltpu.roll(pe_p[u], HEAD, axis=1)

    units = [(cc, pr, hh) for cc in range(nch) for pr in pairs for hh in range(2)]
    a_top, a_bot, q_lo, v_hi, xw, npow = [], [], [], [], [], []
    zeros_c = jnp.zeros((2 * L, LANES), BF16)
    for cc, pr, hh in units:
        u = (cc, pr)
        if hh == 0:
            c_heads = jnp.concatenate([jnp.where(head_mask2[0], c_p[u], zeros_c),
                                       jnp.where(head_mask2[1], c_p[u], zeros_c)], axis=0)
            a_pair = _dot_nt(q_p[u].astype(BF16), c_heads)
        a_all = a_pair[:, hh * 2 * L:(hh + 1) * 2 * L]
        a_all = jnp.where(tri_mask, a_all, 0.0).astype(BF16)
        a_top.append(a_all[:L])
        a_bot.append(a_all[L:])
        q_lo.append(jnp.where(head_mask2[0], q_p[u] if hh == 0 else q_roll[u], 0.0))
        v_hi.append(jnp.where(head_mask1[1], v_p[u] if hh == 1 else v_roll[u], 0.0).astype(BF16))
    for i in range(len(units)):
        zv = jnp.concatenate([zeros_b, v_hi[i]], axis=0)
        xw.append(q_lo[i][:L] + _dot(a_top[i], zv))
        npow.append(a_top[i])
    for it in range(6):
        lo = {4: L // 4, 5: L // 2}.get(it, 0)
        for i in range(len(units)):
            xb = xw[i].astype(BF16)
            if it < 4:
                rhs = jnp.concatenate(
                    [jnp.concatenate([xb, npow[i]], axis=1),
                     jnp.zeros((L, 2 * LANES), BF16)], axis=0)
                res = _dot(npow[i], rhs)
                xw[i] = xw[i] + res[:, :LANES]
                npow[i] = res[:, LANES:].astype(BF16)
            elif it == 4:
                keep = L - lo
                rhs = jnp.concatenate(
                    [jnp.concatenate([xb[:keep], npow[i][:keep]], axis=1),
                     jnp.zeros((2 * L - keep, 2 * LANES), BF16)], axis=0)
                res = _dot(npow[i][lo:], rhs)
                xw[i] = jnp.concatenate([xw[i][:lo], xw[i][lo:] + res[:, :LANES]], axis=0)
                npow[i] = res[L // 2 - lo:, LANES:].astype(BF16)
            else:
                keep = L - lo
                rhs = jnp.concatenate([xb[:keep], jnp.zeros((2 * L - keep, LANES), BF16)], axis=0)
                res = _dot(npow[i], rhs)
                xw[i] = jnp.concatenate([xw[i][:lo], xw[i][lo:] + res], axis=0)
    lhs2 = {}
    for i, (cc, pr, hh) in enumerate(units):
        u = (cc, pr)
        m2 = jnp.concatenate([xw[i].astype(BF16), v_hi[i]], axis=0)
        both = _dot(jnp.concatenate([a_bot[i], d_t[u][hh * HEAD:(hh + 1) * HEAD, :]], axis=0), m2)
        ry = both[:L] + q_lo[i][L:]
        gh = both[L:]
        gh = gh + jnp.where(diag_mask, pe_p[u] if hh == 0 else pe_roll[u], 0.0)
        lhs2[(cc, pr, hh)] = jnp.concatenate([ry, gh], axis=0).astype(BF16)

    eye_rows = [jnp.where(head_mask1[hh], eye_pair, 0.0).astype(BF16) for hh in range(2)]
    n_pairs = N_HEADS // 2
    st = [st_ref[i] for i in range(nb * n_pairs)]
    for c2 in range(cpb):
        for g in range(nb):
            cc = g * cpb + c2
            for pr in pairs:
                si = g * n_pairs + pr
                rhs2 = jnp.concatenate(
                    [jnp.where(head_mask1[0], st[si], 0.0).astype(BF16), eye_rows[0],
                     jnp.where(head_mask1[1], st[si], 0.0).astype(BF16), eye_rows[1]], axis=0)
                out = _dot(jnp.concatenate([lhs2[(cc, pr, 0)], lhs2[(cc, pr, 1)]], axis=1), rhs2)
                y_ref[pl.ds(cc * L, L), sls[pr]] = out[:L]
                st[si] = out[L:]
    for i in range(nb * n_pairs):
        st_ref[i] = st[i]

    y = y_ref[...]
    half = D_RWKV // 2
    ones_half = ones_ref[0:half, 0:half]

    def head_sum(t):
        tb = t.astype(BF16)
        return jnp.concatenate([_dot(tb[:, :half], ones_half), _dot(tb[:, half:], ones_half)], axis=1)

    inv_n = 1.0 / HEAD
    mu = head_sum(y) * inv_n
    d = y - mu
    var = head_sum(d * d) * inv_n
    yn = d * lax.rsqrt(var + GN_EPS) * lng_ref[...] + lnb_ref[...]
    bonus = head_sum(whole(r_ref) * whole(k_ref) * rk_ref[...]) * whole(v_ref)
    out_all = ((yn + bonus) * whole(g_ref)).astype(BF16)
    for g in range(nb):
        o_ref[0, g, 0] = out_all[g * rc:(g + 1) * rc]


def _rwkv_rec(r, lw, k, v, kk, b, g, rk, lng, lnb, ones, batch, seq):
    rc = REC_CHUNKS * CHUNK
    nb = REC_BATCH
    shape5 = (batch // nb, nb, seq // rc, rc, D_RWKV)
    tok = pl.BlockSpec((1, nb, 1, rc, D_RWKV), lambda bi, ci: (bi, 0, ci, 0, 0))
    full = lambda a: pl.BlockSpec(a.shape, lambda bi, ci: (0,) * a.ndim)
    consts = (rk, lng, lnb, ones)
    out = pl.pallas_call(
        _rwkv_rec_body,
        grid=(batch // nb, seq // rc),
        in_specs=[tok] * 7 + [full(a) for a in consts],
        out_specs=tok,
        out_shape=jax.ShapeDtypeStruct(shape5, BF16),
        scratch_shapes=[pltpu.VMEM((nb * (N_HEADS // 2), HEAD, LANES), F32),
                        pltpu.VMEM((nb * rc, D_RWKV), F32)],
        compiler_params=pltpu.CompilerParams(
            dimension_semantics=("arbitrary", "arbitrary"), vmem_limit_bytes=VMEM_LIMIT),
        name="rwkv_rec",
    )(*[a.reshape(shape5) for a in (r, lw, k, v, kk, b, g)], *consts)
    return out.reshape(batch * seq, D_RWKV)


def _rms(x, g):
    ms = jnp.mean(x * x, axis=-1, keepdims=True)
    return x * lax.rsqrt(ms + RMS_EPS) * g


def _mix_out_body(x_ref, yr_ref, yc_ref, wo_ref, g2_ref, w1_ref, w2_ref, gf_ref, o_ref):
    y_cat = jnp.concatenate([yr_ref[...], yc_ref[...]], axis=1)
    x1 = x_ref[...] + _dot(y_cat, wo_ref[...])
    h = _rms(x1, g2_ref[...]).astype(BF16)
    acc = x1
    cols = D_FF // FF_SPLIT
    for c in range(FF_SPLIT):
        z = _dot(h, w1_ref[:, c * cols:(c + 1) * cols])
        z = jnp.square(jnp.maximum(z, 0.0)).astype(BF16)
        acc = acc + _dot(z, w2_ref[c * cols:(c + 1) * cols, :])
    o_ref[...] = _rms(acc, gf_ref[...])


def _mix_out(x2, yr, yc, wo, g2, w1, w2, gf):
    n = x2.shape[0]
    tm = TM_OUT
    tok = lambda w: pl.BlockSpec((tm, w), lambda i: (i, 0))
    full = lambda a: pl.BlockSpec(a.shape, lambda i: (0,) * a.ndim, pipeline_mode=pl.Buffered(1))
    consts = (wo, g2, w1, w2, gf)
    return pl.pallas_call(
        _mix_out_body,
        grid=(n // tm,),
        in_specs=[tok(D_MODEL), tok(D_RWKV), tok(D_CONV)] + [full(a) for a in consts],
        out_specs=tok(D_MODEL),
        out_shape=jax.ShapeDtypeStruct((n, D_MODEL), F32),
        compiler_params=pltpu.CompilerParams(
            dimension_semantics=("arbitrary",), vmem_limit_bytes=VMEM_LIMIT),
        name="mix_out",
    )(x2, yr, yc, *consts)


def kernel(x, norm_mix_g, w_in, shift_mu, w0, w_decay_up, a0, w_aaa_up, w_gate_up, k_k, k_a, r_k, ln_x_g, ln_x_b, conv_w, conv_b, conv_ln_g, conv_ln_b, w_out, norm_mlp_g, w_ff1, w_ff2, norm_final_g):
    batch, seq, _ = x.shape
    assert norm_mix_g.shape[0] == 1, "single layer"
    assert seq % TM_IN == 0 and seq % (REC_CHUNKS * CHUNK) == 0 and batch % REC_BATCH == 0 and (batch * seq) % TM_OUT == 0
    row = lambda a: a.reshape(1, -1)
    x2 = x.reshape(batch * seq, D_MODEL)

    zer = jnp.zeros((D_DECAY_LORA, D_RWKV), F32)
    wwa = jnp.concatenate(
        [jnp.concatenate([w_decay_up[0], zer], axis=1),
         jnp.concatenate([zer, w_aaa_up[0]], axis=1)], axis=0).astype(BF16)
    hid = jnp.arange(D_RWKV) // HEAD
    ones = (hid[:, None] == hid[None, :]).astype(BF16)

    r, lw, k, v, kk, b, g, yc = _mix_in(
        x2, row(norm_mix_g[0]), w_in[0].astype(BF16), row(shift_mu[0]), row(w0[0]), wwa,
        row(a0[0]), w_gate_up[0].astype(BF16), row(k_k[0]), row(k_a[0]), ones,
        jnp.broadcast_to(conv_w[0][:, None, :], (CONV_WIDTH, SUBLANES, D_CONV)),
        row(conv_b[0]), row(conv_ln_g[0]), row(conv_ln_b[0]), seq)
    yr = _rwkv_rec(r, lw, k, v, kk, b, g, row(r_k[0]), row(ln_x_g[0]), row(ln_x_b[0]), ones,
                   batch, seq)
    out = _mix_out(x2, yr, yc, w_out[0].astype(BF16), row(norm_mlp_g[0]),
                   w_ff1[0].astype(BF16), w_ff2[0].astype(BF16), row(norm_final_g))
    return out.reshape(batch, seq, D_MODEL)
```

```python
import functools

import jax
import jax.numpy as jnp
from jax import lax
from jax.experimental import pallas as pl
from jax.experimental.pallas import tpu as pltpu

F32 = jnp.float32
BF16 = jnp.bfloat16

D_MODEL = 1024
D_RWKV = 512
HEAD = 64
N_HEADS = D_RWKV // HEAD
D_CONV = 512
CONV_WIDTH = 31
D_DECAY_LORA = 64
D_AAA_LORA = 64
D_GATE_LORA = 128
D_FF = 4 * D_MODEL
D_RWKV_IN = 3 * D_RWKV + D_DECAY_LORA + D_AAA_LORA + D_GATE_LORA
D_IN = D_RWKV_IN + 2 * D_CONV
RMS_EPS = 1e-6
GN_EPS = 64e-5
LN_EPS = 1e-5
KK_NORM_FLOOR = 1e-12
DECAY_SCALE = 0.6065306597126334
LOG2_E = 1.4426950408889634

LANES = 128
SUBLANES = 8
CHUNK = 64
REC_CHUNKS = 2
REC_BATCH = 4
CONV_HIST = 40
CONV_ROWS = 32
TM_IN = 512
TM_OUT = 1024
FF_SPLIT = 4
VMEM_LIMIT = 56 * 1024 * 1024


def _sigmoid(z):
    return 1.0 / (1.0 + jnp.exp2(z * -LOG2_E))


def _dot(a, b):
    return jnp.dot(a, b, preferred_element_type=F32)


def _dot_nt(a, b):
    return lax.dot_general(a, b, (((1,), (1,)), ((), ())), preferred_element_type=F32)


def _mix_in_body(tiles_per_seq,
                 x_ref, g_ref, win_ref, mu_ref, w0_ref, wwa_ref, a0_ref, wg_ref,
                 kkp_ref, ka_ref, ones_ref, cw_ref, cb_ref, clg_ref, clb_ref,
                 r_out, lw_out, k_out, v_out, kk_out, b_out, g_out, yc_out,
                 carry_ref, hbuf_ref, odd_ref):
    tm = x_ref.shape[0]

    @pl.when(pl.program_id(0) % tiles_per_seq == 0)
    def _():
        carry_ref[...] = jnp.zeros_like(carry_ref)
        hbuf_ref[0, 0:CONV_HIST, :] = jnp.zeros((CONV_HIST, D_CONV), F32)

    x = x_ref[...]
    ms = jnp.mean(x * x, axis=-1, keepdims=True)
    h = (x * lax.rsqrt(ms + RMS_EPS) * g_ref[...]).astype(BF16)
    p = _dot(h, win_ref[...])

    pr = p[:, :D_RWKV_IN]
    rolled = pltpu.roll(pr, 1, axis=0)
    row = lax.broadcasted_iota(jnp.int32, (SUBLANES, D_RWKV_IN), 0)
    last = carry_ref[SUBLANES - 1:SUBLANES, :]
    first = jnp.where(row == 0, last, rolled[:SUBLANES])
    prev = jnp.concatenate([first, rolled[SUBLANES:]], axis=0)
    carry_ref[...] = pr[tm - SUBLANES:tm, :]
    pm = pr + (prev - pr) * mu_ref[...]

    r = pm[:, 0:D_RWKV]
    k = pm[:, D_RWKV:2 * D_RWKV]
    v = pm[:, 2 * D_RWKV:3 * D_RWKV]
    lora_in = pm[:, 3 * D_RWKV:3 * D_RWKV + LANES]
    g_down = pm[:, 3 * D_RWKV + LANES:D_RWKV_IN]

    lane = lax.broadcasted_iota(jnp.int32, lora_in.shape, 1)
    lora_act = jnp.where(lane < D_DECAY_LORA, jnp.tanh(lora_in), lora_in).astype(BF16)
    wa = _dot(lora_act, wwa_ref[...])
    lw_out[...] = -DECAY_SCALE * _sigmoid(w0_ref[...] + wa[:, :D_RWKV])
    a_sig = _sigmoid(a0_ref[...] + wa[:, D_RWKV:])
    g_out[...] = _dot(_sigmoid(g_down).astype(BF16), wg_ref[...])

    kk = k * kkp_ref[...]
    ss = _dot((kk * kk).astype(BF16), ones_ref[...])
    kk = kk * lax.rsqrt(jnp.maximum(ss, KK_NORM_FLOOR * KK_NORM_FLOOR))
    r_out[...] = r
    ka = ka_ref[...]
    k_out[...] = k * (a_sig * ka + (1.0 - ka))
    v_out[...] = v
    kk_out[...] = kk
    b_out[...] = kk * a_sig

    u1 = p[:, D_RWKV_IN:D_RWKV_IN + D_CONV]
    u2 = p[:, D_RWKV_IN + D_CONV:D_IN]
    hbuf_ref[0, CONV_HIST:CONV_HIST + tm, :] = u1 * _sigmoid(u2)
    span = tm + CONV_HIST - SUBLANES
    groups = (CONV_HIST + tm) // SUBLANES
    glu_rows = hbuf_ref[0].reshape(groups, SUBLANES, D_CONV)
    sub = lax.broadcasted_iota(jnp.int32, (groups - 1, SUBLANES, D_CONV), 1)
    for s in range(2, SUBLANES, 2):
        mixed = jnp.where(sub >= s, glu_rows[:-1], glu_rows[1:])
        hbuf_ref[s // 2, 0:span, :] = pltpu.roll(mixed, SUBLANES - s, axis=1).reshape(span, D_CONV)
    base = CONV_HIST - (CONV_WIDTH - 1)

    def taps(acc, first, rows, parity):
        for j in range(parity, CONV_WIDTH, 2):
            off = base + j + parity
            s = off % SUBLANES
            wj = jnp.concatenate([cw_ref[j]] * (rows // SUBLANES), axis=0)
            term = wj * hbuf_ref[s // 2, pl.ds(first + off - s, rows), :]
            acc = term if acc is None else acc + term
        return acc

    for q0 in range(0, tm + SUBLANES, CONV_ROWS):
        rows = min(CONV_ROWS, tm + SUBLANES - q0)
        odd_ref[q0:q0 + rows, :] = taps(None, q0 - SUBLANES, rows, 1)
    blk_groups = CONV_ROWS // SUBLANES
    sub_blk = lax.broadcasted_iota(jnp.int32, (blk_groups, SUBLANES, D_CONV), 1)
    for r0 in range(0, tm, CONV_ROWS):
        acc = taps(jnp.broadcast_to(cb_ref[...], (CONV_ROWS, D_CONV)), r0, CONV_ROWS, 0)
        og = odd_ref[r0:r0 + CONV_ROWS + SUBLANES, :].reshape(blk_groups + 1, SUBLANES, D_CONV)
        og = jnp.where(sub_blk >= SUBLANES - 1, og[:-1], og[1:])
        acc = acc + pltpu.roll(og, 1, axis=1).reshape(CONV_ROWS, D_CONV)
        mu = jnp.mean(acc, axis=-1, keepdims=True)
        d = acc - mu
        var = jnp.mean(d * d, axis=-1, keepdims=True)
        hn = d * lax.rsqrt(var + LN_EPS) * clg_ref[...] + clb_ref[...]
        yc_out[r0:r0 + CONV_ROWS, :] = (hn * _sigmoid(hn)).astype(BF16)
    hbuf_ref[0, 0:CONV_HIST, :] = hbuf_ref[0, tm:tm + CONV_HIST, :]


def _mix_in(x2, g, win, mu, w0, wwa, a0, wg, kkp, ka, ones, cw, cb, clg, clb, seq):
    n = x2.shape[0]
    tm = TM_IN
    full = lambda a: pl.BlockSpec(a.shape, lambda i: (0,) * a.ndim, pipeline_mode=pl.Buffered(1))
    tok = lambda w: pl.BlockSpec((tm, w), lambda i: (i, 0))
    consts = (g, win, mu, w0, wwa, a0, wg, kkp, ka, ones, cw, cb, clg, clb)
    out_f32 = jax.ShapeDtypeStruct((n, D_RWKV), F32)
    return pl.pallas_call(
        functools.partial(_mix_in_body, seq // tm),
        grid=(n // tm,),
        in_specs=[tok(D_MODEL)] + [full(a) for a in consts],
        out_specs=[tok(D_RWKV)] * 7 + [tok(D_CONV)],
        out_shape=[out_f32] * 7 + [jax.ShapeDtypeStruct((n, D_CONV), BF16)],
        scratch_shapes=[pltpu.VMEM((SUBLANES, D_RWKV_IN), F32),
                        pltpu.VMEM((SUBLANES // 2, CONV_HIST + tm, D_CONV), F32),
                        pltpu.VMEM((tm + SUBLANES, D_CONV), F32)],
        compiler_params=pltpu.CompilerParams(
            dimension_semantics=("arbitrary",), vmem_limit_bytes=VMEM_LIMIT),
        name="mix_in",
    )(x2, *consts)


def _rwkv_rec_body(r_ref, lw_ref, k_ref, v_ref, kk_ref, b_ref, g_ref,
                   rk_ref, lng_ref, lnb_ref, mean_ref,
                   o_ref, st_ref, y_ref):
    L = CHUNK
    nb, rc = r_ref.shape[1], r_ref.shape[3]
    cpb = rc // L
    nch = nb * cpb

    def chunk(ref, vc):
        return ref[0, vc // cpb, 0, pl.ds((vc % cpb) * L, L), :]

    def whole(ref):
        return jnp.concatenate([ref[0, g, 0] for g in range(nb)], axis=0)

    @pl.when(pl.program_id(1) == 0)
    def _():
        st_ref[...] = jnp.zeros_like(st_ref)

    ri = lax.broadcasted_iota(jnp.int32, (L, L), 0)
    ci = lax.broadcasted_iota(jnp.int32, (L, L), 1)
    tri = jnp.where(ri >= ci, 1.0, 0.0).astype(BF16)
    last_row = lax.broadcasted_iota(jnp.int32, (L, D_RWKV), 0) == L - 1
    lane = lax.broadcasted_iota(jnp.int32, (2 * L, LANES), 1)
    rowi = lax.broadcasted_iota(jnp.int32, (2 * L, LANES), 0)
    head_mask2 = (lane < HEAD, lane >= HEAD)
    t_idx = rowi & (L - 1)
    j_idx = lane & (L - 1)
    tri_mask = (j_idx < t_idx) | ((rowi >= L) & (j_idx == t_idx))
    lane1 = lax.broadcasted_iota(jnp.int32, (L, LANES), 1)
    row1 = lax.broadcasted_iota(jnp.int32, (L, LANES), 0)
    head_mask1 = (lane1 < HEAD, lane1 >= HEAD)
    diag_mask = row1 == lane1
    eye_pair = jnp.where(row1 == (lane1 & (L - 1)), 1.0, 0.0)
    zeros_b = jnp.zeros((L, LANES), BF16)
    pairs = range(N_HEADS // 2)
    sls = [slice(pr * LANES, (pr + 1) * LANES) for pr in pairs]

    q_p, c_p, d_t, q_roll, v_p, v_roll, pe_p, pe_roll = ({} for _ in range(8))
    for cc in range(nch):
        lw = chunk(lw_ref, cc)
        lw_hi = lw.astype(BF16)
        lw_lo = (lw - lw_hi.astype(F32)).astype(BF16)
        lp = _dot(tri, lw_hi) + _dot(tri, lw_lo)
        lp_end = jnp.sum(jnp.where(last_row, lp, 0.0), axis=0, keepdims=True)
        p_end = jnp.exp(lp_end)
        k = chunk(k_ref, cc)
        v = chunk(v_ref, cc)
        b = chunk(b_ref, cc)
        e_neg = jnp.exp2(lp * -LOG2_E)
        e_end = p_end * e_neg
        at = -(chunk(kk_ref, cc) * jnp.exp(lp - lw))
        rt = chunk(r_ref, cc) * jnp.exp(lp)
        bt = b * e_neg
        kt = k * e_neg
        bp = b * e_end
        kp = k * e_end
        for pr in pairs:
            sl = sls[pr]
            u = (cc, pr)
            q_p[u] = jnp.concatenate([at[:, sl], rt[:, sl]], axis=0)
            c_p[u] = jnp.concatenate([bt[:, sl], kt[:, sl]], axis=0).astype(BF16)
            d_t[u] = jnp.concatenate([bp[:, sl], kp[:, sl]], axis=0).T.astype(BF16)
            q_roll[u] = pltpu.roll(q_p[u], HEAD, axis=1)
            v_p[u] = v[:, sl]
            v_roll[u] = pltpu.roll(v_p[u], HEAD, axis=1)
            pe_p[u] = p_end[:, sl]
            pe_roll[u] = pltpu.roll(pe_p[u], HEAD, axis=1)

    units = [(cc, pr, hh) for cc in range(nch) for pr in pairs for hh in range(2)]
    a_top, a_bot, q_lo, v_hi, xw, npow = [], [], [], [], [], []
    zeros_c = jnp.zeros((2 * L, LANES), BF16)
    for cc, pr, hh in units:
        u = (cc, pr)
        if hh == 0:
            c_heads = jnp.concatenate([jnp.where(head_mask2[0], c_p[u], zeros_c),
                                       jnp.where(head_mask2[1], c_p[u], zeros_c)], axis=0)
            a_pair = _dot_nt(q_p[u].astype(BF16), c_heads)
        a_all = a_pair[:, hh * 2 * L:(hh + 1) * 2 * L]
        a_all = jnp.where(tri_mask, a_all, 0.0).astype(BF16)
        a_top.append(a_all[:L])
        a_bot.append(a_all[L:])
        q_lo.append(jnp.where(head_mask2[0], q_p[u] if hh == 0 else q_roll[u], 0.0))
        v_hi.append(jnp.where(head_mask1[1], v_p[u] if hh == 1 else v_roll[u], 0.0).astype(BF16))
    for i in range(len(units)):
        zv = jnp.concatenate([zeros_b, v_hi[i]], axis=0)
        xw.append(q_lo[i][:L] + _dot(a_top[i], zv))
        npow.append(a_top[i])
    for it in range(6):
        lo = {4: L // 4, 5: L // 2}.get(it, 0)
        for i in range(len(units)):
            xb = xw[i].astype(BF16)
            if it < 4:
                rhs = jnp.concatenate(
                    [jnp.concatenate([xb, npow[i]], axis=1),
                     jnp.zeros((L, 2 * LANES), BF16)], axis=0)
                res = _dot(npow[i], rhs)
                xw[i] = xw[i] + res[:, :LANES]
                npow[i] = res[:, LANES:].astype(BF16)
            elif it == 4:
                keep = L - lo
                rhs = jnp.concatenate(
                    [jnp.concatenate([xb[:keep], npow[i][:keep]], axis=1),
                     jnp.zeros((2 * L - keep, 2 * LANES), BF16)], axis=0)
                res = _dot(npow[i][lo:], rhs)
                xw[i] = jnp.concatenate([xw[i][:lo], xw[i][lo:] + res[:, :LANES]], axis=0)
                npow[i] = res[L // 2 - lo:, LANES:].astype(BF16)
            else:
                keep = L - lo
                rhs = jnp.concatenate([xb[:keep], jnp.zeros((2 * L - keep, LANES), BF16)], axis=0)
                res = _dot(npow[i], rhs)
                xw[i] = jnp.concatenate([xw[i][:lo], xw[i][lo:] + res], axis=0)
    lhs2 = {}
    for i, (cc, pr, hh) in enumerate(units):
        u = (cc, pr)
        m2 = jnp.concatenate([xw[i].astype(BF16), v_hi[i]], axis=0)
        both = _dot(jnp.concatenate([a_bot[i], d_t[u][hh * HEAD:(hh + 1) * HEAD, :]], axis=0), m2)
        ry = both[:L] + q_lo[i][L:]
        gh = both[L:]
        gh = gh + jnp.where(diag_mask, pe_p[u] if hh == 0 else pe_roll[u], 0.0)
        lhs2[(cc, pr, hh)] = jnp.concatenate([ry, gh], axis=0).astype(BF16)

    eye_rows = [jnp.where(head_mask1[hh], eye_pair, 0.0).astype(BF16) for hh in range(2)]
    n_pairs = N_HEADS // 2
    st = [st_ref[i] for i in range(nb * n_pairs)]
    for c2 in range(cpb):
        for g in range(nb):
            cc = g * cpb + c2
            for pr in pairs:
                si = g * n_pairs + pr
                rhs2 = jnp.concatenate(
                    [jnp.where(head_mask1[0], st[si], 0.0).astype(BF16), eye_rows[0],
                     jnp.where(head_mask1[1], st[si], 0.0).astype(BF16), eye_rows[1]], axis=0)
                out = _dot(jnp.concatenate([lhs2[(cc, pr, 0)], lhs2[(cc, pr, 1)]], axis=1), rhs2)
                y_ref[pl.ds(cc * L, L), sls[pr]] = out[:L]
                st[si] = out[L:]
    for i in range(nb * n_pairs):
        st_ref[i] = st[i]

    y = y_ref[...]
    half = D_RWKV // 2
    mean_half = mean_ref[0:half, 0:half]

    def head_mean(t):
        tb = t.astype(BF16)
        return jnp.concatenate([_dot(tb[:, :half], mean_half), _dot(tb[:, half:], mean_half)], axis=1)

    mu = head_mean(y)
    d = y - mu
    var = head_mean(d * d)
    yn = d * lax.rsqrt(var + GN_EPS) * lng_ref[...] + lnb_ref[...]
    bonus = head_mean(whole(r_ref) * whole(k_ref) * (rk_ref[...] * float(HEAD))) * whole(v_ref)
    out_all = ((yn + bonus) * whole(g_ref)).astype(BF16)
    for g in range(nb):
        o_ref[0, g, 0] = out_all[g * rc:(g + 1) * rc]


def _rwkv_rec(r, lw, k, v, kk, b, g, rk, lng, lnb, head_mean, batch, seq):
    rc = REC_CHUNKS * CHUNK
    nb = REC_BATCH
    shape5 = (batch // nb, nb, seq // rc, rc, D_RWKV)
    tok = pl.BlockSpec((1, nb, 1, rc, D_RWKV), lambda bi, ci: (bi, 0, ci, 0, 0))
    full = lambda a: pl.BlockSpec(a.shape, lambda bi, ci: (0,) * a.ndim)
    consts = (rk, lng, lnb, head_mean)
    out = pl.pallas_call(
        _rwkv_rec_body,
        grid=(batch // nb, seq // rc),
        in_specs=[tok] * 7 + [full(a) for a in consts],
        out_specs=tok,
        out_shape=jax.ShapeDtypeStruct(shape5, BF16),
        scratch_shapes=[pltpu.VMEM((nb * (N_HEADS // 2), HEAD, LANES), F32),
                        pltpu.VMEM((nb * rc, D_RWKV), F32)],
        compiler_params=pltpu.CompilerParams(
            dimension_semantics=("arbitrary", "arbitrary"), vmem_limit_bytes=VMEM_LIMIT),
        name="rwkv_rec",
    )(*[a.reshape(shape5) for a in (r, lw, k, v, kk, b, g)], *consts)
    return out.reshape(batch * seq, D_RWKV)


def _rms(x, g):
    ms = jnp.mean(x * x, axis=-1, keepdims=True)
    return x * lax.rsqrt(ms + RMS_EPS) * g


def _mix_out_body(x_ref, yr_ref, yc_ref, wo_ref, g2_ref, w1_ref, w2_ref, gf_ref, o_ref):
    y_cat = jnp.concatenate([yr_ref[...], yc_ref[...]], axis=1)
    x1 = x_ref[...] + _dot(y_cat, wo_ref[...])
    h = _rms(x1, g2_ref[...]).astype(BF16)
    acc = x1
    cols = D_FF // FF_SPLIT
    for c in range(FF_SPLIT):
        z = _dot(h, w1_ref[:, c * cols:(c + 1) * cols])
        z = jnp.square(jnp.maximum(z, 0.0)).astype(BF16)
        acc = acc + _dot(z, w2_ref[c * cols:(c + 1) * cols, :])
    o_ref[...] = _rms(acc, gf_ref[...])


def _mix_out(x2, yr, yc, wo, g2, w1, w2, gf):
    n = x2.shape[0]
    tm = TM_OUT
    tok = lambda w: pl.BlockSpec((tm, w), lambda i: (i, 0))
    full = lambda a: pl.BlockSpec(a.shape, lambda i: (0,) * a.ndim, pipeline_mode=pl.Buffered(1))
    consts = (wo, g2, w1, w2, gf)
    return pl.pallas_call(
        _mix_out_body,
        grid=(n // tm,),
        in_specs=[tok(D_MODEL), tok(D_RWKV), tok(D_CONV)] + [full(a) for a in consts],
        out_specs=tok(D_MODEL),
        out_shape=jax.ShapeDtypeStruct((n, D_MODEL), F32),
        compiler_params=pltpu.CompilerParams(
            dimension_semantics=("arbitrary",), vmem_limit_bytes=VMEM_LIMIT),
        name="mix_out",
    )(x2, yr, yc, *consts)


def kernel(x, norm_mix_g, w_in, shift_mu, w0, w_decay_up, a0, w_aaa_up, w_gate_up, k_k, k_a, r_k, ln_x_g, ln_x_b, conv_w, conv_b, conv_ln_g, conv_ln_b, w_out, norm_mlp_g, w_ff1, w_ff2, norm_final_g):
    batch, seq, _ = x.shape
    assert norm_mix_g.shape[0] == 1, "single layer"
    assert seq % TM_IN == 0 and seq % (REC_CHUNKS * CHUNK) == 0 and batch % REC_BATCH == 0 and (batch * seq) % TM_OUT == 0
    row = lambda a: a.reshape(1, -1)
    x2 = x.reshape(batch * seq, D_MODEL)

    zer = jnp.zeros((D_DECAY_LORA, D_RWKV), F32)
    wwa = jnp.concatenate(
        [jnp.concatenate([w_decay_up[0], zer], axis=1),
         jnp.concatenate([zer, w_aaa_up[0]], axis=1)], axis=0).astype(BF16)
    hid = jnp.arange(D_RWKV) // HEAD
    ones = (hid[:, None] == hid[None, :]).astype(BF16)

    r, lw, k, v, kk, b, g, yc = _mix_in(
        x2, row(norm_mix_g[0]), w_in[0].astype(BF16), row(shift_mu[0]), row(w0[0]), wwa,
        row(a0[0]), w_gate_up[0].astype(BF16), row(k_k[0]), row(k_a[0]), ones,
        jnp.broadcast_to(conv_w[0][:, None, :], (CONV_WIDTH, SUBLANES, D_CONV)),
        row(conv_b[0]), row(conv_ln_g[0]), row(conv_ln_b[0]), seq)
    yr = _rwkv_rec(r, lw, k, v, kk, b, g, row(r_k[0]), row(ln_x_g[0]), row(ln_x_b[0]),
                   ones * (1.0 / HEAD),
                   batch, seq)
    out = _mix_out(x2, yr, yc, w_out[0].astype(BF16), row(norm_mlp_g[0]),
                   w_ff1[0].astype(BF16), w_ff2[0].astype(BF16), row(norm_final_g))
    return out.reshape(batch, seq, D_MODEL)
```

```python
import functools

import jax
import jax.numpy as jnp
from jax import lax
from jax.experimental import pallas as pl
from jax.experimental.pallas import tpu as pltpu

F32 = jnp.float32
BF16 = jnp.bfloat16

D_MODEL = 1024
D_RWKV = 512
HEAD = 64
N_HEADS = D_RWKV // HEAD
D_CONV = 512
CONV_WIDTH = 31
D_DECAY_LORA = 64
D_AAA_LORA = 64
D_GATE_LORA = 128
D_FF = 4 * D_MODEL
D_RWKV_IN = 3 * D_RWKV + D_DECAY_LORA + D_AAA_LORA + D_GATE_LORA
D_IN = D_RWKV_IN + 2 * D_CONV
RMS_EPS = 1e-6
GN_EPS = 64e-5
LN_EPS = 1e-5
KK_NORM_FLOOR = 1e-12
DECAY_SCALE = 0.6065306597126334
LOG2_E = 1.4426950408889634

LANES = 128
SUBLANES = 8
CHUNK = 64
REC_CHUNKS = 2
REC_BATCH = 4
CONV_HIST = 40
CONV_ROWS = 32
TM_IN = 512
TM_OUT = 1024
FF_SPLIT = 4
OUT_PARTS = 4
VMEM_LIMIT = 56 * 1024 * 1024


def _sigmoid(z):
    return 1.0 / (1.0 + jnp.exp2(z * -LOG2_E))


def _dot(a, b):
    return jnp.dot(a, b, preferred_element_type=F32)


def _dot_nt(a, b):
    return lax.dot_general(a, b, (((1,), (1,)), ((), ())), preferred_element_type=F32)


def _mix_in_body(tiles_per_seq,
                 x_ref, g_ref, win_ref, mu_ref, w0_ref, wwa_ref, a0_ref, wg_ref,
                 kkp_ref, ka_ref, ones_ref, cw_ref, cb_ref, clg_ref, clb_ref,
                 r_out, lw_out, k_out, v_out, kk_out, b_out, g_out, yc_out,
                 carry_ref, hbuf_ref, odd_ref):
    tm = x_ref.shape[0]

    @pl.when(pl.program_id(0) % tiles_per_seq == 0)
    def _():
        carry_ref[...] = jnp.zeros_like(carry_ref)
        hbuf_ref[0, 0:CONV_HIST, :] = jnp.zeros((CONV_HIST, D_CONV), F32)

    x = x_ref[...]
    ms = jnp.mean(x * x, axis=-1, keepdims=True)
    h = (x * lax.rsqrt(ms + RMS_EPS) * g_ref[...]).astype(BF16)
    p = _dot(h, win_ref[...])

    pr = p[:, :D_RWKV_IN]
    rolled = pltpu.roll(pr, 1, axis=0)
    row = lax.broadcasted_iota(jnp.int32, (SUBLANES, D_RWKV_IN), 0)
    last = carry_ref[SUBLANES - 1:SUBLANES, :]
    first = jnp.where(row == 0, last, rolled[:SUBLANES])
    prev = jnp.concatenate([first, rolled[SUBLANES:]], axis=0)
    carry_ref[...] = pr[tm - SUBLANES:tm, :]
    pm = pr + (prev - pr) * mu_ref[...]

    r = pm[:, 0:D_RWKV]
    k = pm[:, D_RWKV:2 * D_RWKV]
    v = pm[:, 2 * D_RWKV:3 * D_RWKV]
    lora_in = pm[:, 3 * D_RWKV:3 * D_RWKV + LANES]
    g_down = pm[:, 3 * D_RWKV + LANES:D_RWKV_IN]

    lane = lax.broadcasted_iota(jnp.int32, lora_in.shape, 1)
    lora_act = jnp.where(lane < D_DECAY_LORA, jnp.tanh(lora_in), lora_in).astype(BF16)
    wa = _dot(lora_act, wwa_ref[...])
    lw_out[...] = -DECAY_SCALE * _sigmoid(w0_ref[...] + wa[:, :D_RWKV])
    a_sig = _sigmoid(a0_ref[...] + wa[:, D_RWKV:])
    g_out[...] = _dot(_sigmoid(g_down).astype(BF16), wg_ref[...])

    kk = k * kkp_ref[...]
    ss = _dot((kk * kk).astype(BF16), ones_ref[...])
    kk = kk * lax.rsqrt(jnp.maximum(ss, KK_NORM_FLOOR * KK_NORM_FLOOR))
    r_out[...] = r
    ka = ka_ref[...]
    k_out[...] = k * (a_sig * ka + (1.0 - ka))
    v_out[...] = v
    kk_out[...] = kk
    b_out[...] = kk * a_sig

    u1 = p[:, D_RWKV_IN:D_RWKV_IN + D_CONV]
    u2 = p[:, D_RWKV_IN + D_CONV:D_IN]
    hbuf_ref[0, CONV_HIST:CONV_HIST + tm, :] = u1 * _sigmoid(u2)
    span = tm + CONV_HIST - SUBLANES
    groups = (CONV_HIST + tm) // SUBLANES
    glu_rows = hbuf_ref[0].reshape(groups, SUBLANES, D_CONV)
    sub = lax.broadcasted_iota(jnp.int32, (groups - 1, SUBLANES, D_CONV), 1)
    for s in range(2, SUBLANES, 2):
        mixed = jnp.where(sub >= s, glu_rows[:-1], glu_rows[1:])
        hbuf_ref[s // 2, 0:span, :] = pltpu.roll(mixed, SUBLANES - s, axis=1).reshape(span, D_CONV)
    base = CONV_HIST - (CONV_WIDTH - 1)

    def taps(acc, first, rows, parity):
        for j in range(parity, CONV_WIDTH, 2):
            off = base + j + parity
            s = off % SUBLANES
            wj = jnp.concatenate([cw_ref[j]] * (rows // SUBLANES), axis=0)
            term = wj * hbuf_ref[s // 2, pl.ds(first + off - s, rows), :]
            acc = term if acc is None else acc + term
        return acc

    for q0 in range(0, tm + SUBLANES, CONV_ROWS):
        rows = min(CONV_ROWS, tm + SUBLANES - q0)
        odd_ref[q0:q0 + rows, :] = taps(None, q0 - SUBLANES, rows, 1)
    blk_groups = CONV_ROWS // SUBLANES
    sub_blk = lax.broadcasted_iota(jnp.int32, (blk_groups, SUBLANES, D_CONV), 1)
    for r0 in range(0, tm, CONV_ROWS):
        acc = taps(jnp.broadcast_to(cb_ref[...], (CONV_ROWS, D_CONV)), r0, CONV_ROWS, 0)
        og = odd_ref[r0:r0 + CONV_ROWS + SUBLANES, :].reshape(blk_groups + 1, SUBLANES, D_CONV)
        og = jnp.where(sub_blk >= SUBLANES - 1, og[:-1], og[1:])
        acc = acc + pltpu.roll(og, 1, axis=1).reshape(CONV_ROWS, D_CONV)
        mu = jnp.mean(acc, axis=-1, keepdims=True)
        d = acc - mu
        var = jnp.mean(d * d, axis=-1, keepdims=True)
        hn = d * lax.rsqrt(var + LN_EPS) * clg_ref[...] + clb_ref[...]
        yc_out[r0:r0 + CONV_ROWS, :] = (hn * _sigmoid(hn)).astype(BF16)
    hbuf_ref[0, 0:CONV_HIST, :] = hbuf_ref[0, tm:tm + CONV_HIST, :]


def _mix_in(x2, g, win, mu, w0, wwa, a0, wg, kkp, ka, ones, cw, cb, clg, clb, seq):
    n = x2.shape[0]
    tm = TM_IN
    full = lambda a: pl.BlockSpec(a.shape, lambda i: (0,) * a.ndim, pipeline_mode=pl.Buffered(1))
    tok = lambda w: pl.BlockSpec((tm, w), lambda i: (i, 0))
    consts = (g, win, mu, w0, wwa, a0, wg, kkp, ka, ones, cw, cb, clg, clb)
    out_f32 = jax.ShapeDtypeStruct((n, D_RWKV), F32)
    return pl.pallas_call(
        functools.partial(_mix_in_body, seq // tm),
        grid=(n // tm,),
        in_specs=[tok(D_MODEL)] + [full(a) for a in consts],
        out_specs=[tok(D_RWKV)] * 7 + [tok(D_CONV)],
        out_shape=[out_f32] * 7 + [jax.ShapeDtypeStruct((n, D_CONV), BF16)],
        scratch_shapes=[pltpu.VMEM((SUBLANES, D_RWKV_IN), F32),
                        pltpu.VMEM((SUBLANES // 2, CONV_HIST + tm, D_CONV), F32),
                        pltpu.VMEM((tm + SUBLANES, D_CONV), F32)],
        compiler_params=pltpu.CompilerParams(
            dimension_semantics=("arbitrary",), vmem_limit_bytes=VMEM_LIMIT),
        name="mix_in",
    )(x2, *consts)


def _rwkv_rec_body(r_ref, lw_ref, k_ref, v_ref, kk_ref, b_ref, g_ref,
                   rk_ref, lng_ref, lnb_ref, mean_ref,
                   o_ref, st_ref, y_ref):
    L = CHUNK
    nb, rc = r_ref.shape[1], r_ref.shape[3]
    cpb = rc // L
    nch = nb * cpb

    def chunk(ref, vc):
        return ref[0, vc // cpb, 0, pl.ds((vc % cpb) * L, L), :]

    def whole(ref):
        return jnp.concatenate([ref[0, g, 0] for g in range(nb)], axis=0)

    @pl.when(pl.program_id(1) == 0)
    def _():
        st_ref[...] = jnp.zeros_like(st_ref)

    ri = lax.broadcasted_iota(jnp.int32, (L, L), 0)
    ci = lax.broadcasted_iota(jnp.int32, (L, L), 1)
    tri = jnp.where(ri >= ci, 1.0, 0.0).astype(BF16)
    last_row = lax.broadcasted_iota(jnp.int32, (L, D_RWKV), 0) == L - 1
    lane = lax.broadcasted_iota(jnp.int32, (2 * L, LANES), 1)
    rowi = lax.broadcasted_iota(jnp.int32, (2 * L, LANES), 0)
    head_mask2 = (lane < HEAD, lane >= HEAD)
    t_idx = rowi & (L - 1)
    j_idx = lane & (L - 1)
    tri_mask = (j_idx < t_idx) | ((rowi >= L) & (j_idx == t_idx))
    lane1 = lax.broadcasted_iota(jnp.int32, (L, LANES), 1)
    row1 = lax.broadcasted_iota(jnp.int32, (L, LANES), 0)
    head_mask1 = (lane1 < HEAD, lane1 >= HEAD)
    diag_mask = row1 == lane1
    eye_pair = jnp.where(row1 == (lane1 & (L - 1)), 1.0, 0.0)
    zeros_b = jnp.zeros((L, LANES), BF16)
    pairs = range(N_HEADS // 2)
    sls = [slice(pr * LANES, (pr + 1) * LANES) for pr in pairs]

    q_p, c_p, d_t, q_roll, v_p, v_roll, pe_p, pe_roll = ({} for _ in range(8))
    for cc in range(nch):
        lw = chunk(lw_ref, cc)
        lw_hi = lw.astype(BF16)
        lw_lo = (lw - lw_hi.astype(F32)).astype(BF16)
        lp = _dot(tri, lw_hi) + _dot(tri, lw_lo)
        lp_end = jnp.sum(jnp.where(last_row, lp, 0.0), axis=0, keepdims=True)
        p_end = jnp.exp(lp_end)
        k = chunk(k_ref, cc)
        v = chunk(v_ref, cc)
        b = chunk(b_ref, cc)
        e_neg = jnp.exp2(lp * -LOG2_E)
        e_end = p_end * e_neg
        at = -(chunk(kk_ref, cc) * jnp.exp(lp - lw))
        rt = chunk(r_ref, cc) * jnp.exp(lp)
        bt = b * e_neg
        kt = k * e_neg
        bp = b * e_end
        kp = k * e_end
        for pr in pairs:
            sl = sls[pr]
            u = (cc, pr)
            q_p[u] = jnp.concatenate([at[:, sl], rt[:, sl]], axis=0)
            c_p[u] = jnp.concatenate([bt[:, sl], kt[:, sl]], axis=0).astype(BF16)
            d_t[u] = jnp.concatenate([bp[:, sl], kp[:, sl]], axis=0).T.astype(BF16)
            q_roll[u] = pltpu.roll(q_p[u], HEAD, axis=1)
            v_p[u] = v[:, sl]
            v_roll[u] = pltpu.roll(v_p[u], HEAD, axis=1)
            pe_p[u] = p_end[:, sl]
            pe_roll[u] = pltpu.roll(pe_p[u], HEAD, axis=1)

    units = [(cc, pr, hh) for cc in range(nch) for pr in pairs for hh in range(2)]
    a_top, a_bot, q_lo, v_hi, xw, npow = [], [], [], [], [], []
    zeros_c = jnp.zeros((2 * L, LANES), BF16)
    for cc, pr, hh in units:
        u = (cc, pr)
        if hh == 0:
            c_heads = jnp.concatenate([jnp.where(head_mask2[0], c_p[u], zeros_c),
                                       jnp.where(head_mask2[1], c_p[u], zeros_c)], axis=0)
            a_pair = _dot_nt(q_p[u].astype(BF16), c_heads)
        a_all = a_pair[:, hh * 2 * L:(hh + 1) * 2 * L]
        a_all = jnp.where(tri_mask, a_all, 0.0).astype(BF16)
        a_top.append(a_all[:L])
        a_bot.append(a_all[L:])
        q_lo.append(jnp.where(head_mask2[0], q_p[u] if hh == 0 else q_roll[u], 0.0))
        v_hi.append(jnp.where(head_mask1[1], v_p[u] if hh == 1 else v_roll[u], 0.0).astype(BF16))
    for i in range(len(units)):
        zv = jnp.concatenate([zeros_b, v_hi[i]], axis=0)
        xw.append(q_lo[i][:L] + _dot(a_top[i], zv))
        npow.append(a_top[i])
    for it in range(6):
        lo = {4: L // 4, 5: L // 2}.get(it, 0)
        for i in range(len(units)):
            xb = xw[i].astype(BF16)
            if it < 4:
                rhs = jnp.concatenate(
                    [jnp.concatenate([xb, npow[i]], axis=1),
                     jnp.zeros((L, 2 * LANES), BF16)], axis=0)
                res = _dot(npow[i], rhs)
                xw[i] = xw[i] + res[:, :LANES]
                npow[i] = res[:, LANES:].astype(BF16)
            elif it == 4:
                keep = L - lo
                rhs = jnp.concatenate(
                    [jnp.concatenate([xb[:keep], npow[i][:keep]], axis=1),
                     jnp.zeros((2 * L - keep, 2 * LANES), BF16)], axis=0)
                res = _dot(npow[i][lo:], rhs)
                xw[i] = jnp.concatenate([xw[i][:lo], xw[i][lo:] + res[:, :LANES]], axis=0)
                npow[i] = res[L // 2 - lo:, LANES:].astype(BF16)
            else:
                keep = L - lo
                rhs = jnp.concatenate([xb[:keep], jnp.zeros((2 * L - keep, LANES), BF16)], axis=0)
                res = _dot(npow[i], rhs)
                xw[i] = jnp.concatenate([xw[i][:lo], xw[i][lo:] + res], axis=0)
    lhs2 = {}
    for i, (cc, pr, hh) in enumerate(units):
        u = (cc, pr)
        m2 = jnp.concatenate([xw[i].astype(BF16), v_hi[i]], axis=0)
        both = _dot(jnp.concatenate([a_bot[i], d_t[u][hh * HEAD:(hh + 1) * HEAD, :]], axis=0), m2)
        ry = both[:L] + q_lo[i][L:]
        gh = both[L:]
        gh = gh + jnp.where(diag_mask, pe_p[u] if hh == 0 else pe_roll[u], 0.0)
        lhs2[(cc, pr, hh)] = jnp.concatenate([ry, gh], axis=0).astype(BF16)

    eye_rows = [jnp.where(head_mask1[hh], eye_pair, 0.0).astype(BF16) for hh in range(2)]
    n_pairs = N_HEADS // 2
    st = [st_ref[i] for i in range(nb * n_pairs)]
    for c2 in range(cpb):
        for g in range(nb):
            cc = g * cpb + c2
            for pr in pairs:
                si = g * n_pairs + pr
                rhs2 = jnp.concatenate(
                    [jnp.where(head_mask1[0], st[si], 0.0).astype(BF16), eye_rows[0],
                     jnp.where(head_mask1[1], st[si], 0.0).astype(BF16), eye_rows[1]], axis=0)
                out = _dot(jnp.concatenate([lhs2[(cc, pr, 0)], lhs2[(cc, pr, 1)]], axis=1), rhs2)
                y_ref[pl.ds(cc * L, L), sls[pr]] = out[:L]
                st[si] = out[L:]
    for i in range(nb * n_pairs):
        st_ref[i] = st[i]

    y = y_ref[...]
    half = D_RWKV // 2
    mean_half = mean_ref[0:half, 0:half]

    def head_mean(t):
        tb = t.astype(BF16)
        return jnp.concatenate([_dot(tb[:, :half], mean_half), _dot(tb[:, half:], mean_half)], axis=1)

    mu = head_mean(y)
    d = y - mu
    var = head_mean(d * d)
    yn = d * lax.rsqrt(var + GN_EPS) * lng_ref[...] + lnb_ref[...]
    bonus = head_mean(whole(r_ref) * whole(k_ref) * (rk_ref[...] * float(HEAD))) * whole(v_ref)
    out_all = ((yn + bonus) * whole(g_ref)).astype(BF16)
    for g in range(nb):
        o_ref[0, g, 0] = out_all[g * rc:(g + 1) * rc]


def _rwkv_rec(r, lw, k, v, kk, b, g, rk, lng, lnb, head_mean, batch, seq):
    rc = REC_CHUNKS * CHUNK
    nb = REC_BATCH
    shape5 = (batch // nb, nb, seq // rc, rc, D_RWKV)
    tok = pl.BlockSpec((1, nb, 1, rc, D_RWKV), lambda bi, ci: (bi, 0, ci, 0, 0))
    full = lambda a: pl.BlockSpec(a.shape, lambda bi, ci: (0,) * a.ndim)
    consts = (rk, lng, lnb, head_mean)
    out = pl.pallas_call(
        _rwkv_rec_body,
        grid=(batch // nb, seq // rc),
        in_specs=[tok] * 7 + [full(a) for a in consts],
        out_specs=tok,
        out_shape=jax.ShapeDtypeStruct(shape5, BF16),
        scratch_shapes=[pltpu.VMEM((nb * (N_HEADS // 2), HEAD, LANES), F32),
                        pltpu.VMEM((nb * rc, D_RWKV), F32)],
        compiler_params=pltpu.CompilerParams(
            dimension_semantics=("arbitrary", "arbitrary"), vmem_limit_bytes=VMEM_LIMIT),
        name="rwkv_rec",
    )(*[a.reshape(shape5) for a in (r, lw, k, v, kk, b, g)], *consts)
    return out.reshape(batch * seq, D_RWKV)


def _rms(x, g):
    ms = jnp.mean(x * x, axis=-1, keepdims=True)
    return x * lax.rsqrt(ms + RMS_EPS) * g


def _mix_out_body(x_ref, yr_ref, yc_ref, wo_ref, g2_ref, w1_ref, w2_ref, gf_ref, o_ref):
    part = x_ref.shape[0] // OUT_PARTS
    parts = [slice(i * part, (i + 1) * part) for i in range(OUT_PARTS)]
    acc = []
    for rows in parts:
        y_cat = jnp.concatenate([yr_ref[rows, :], yc_ref[rows, :]], axis=1)
        acc.append(x_ref[rows, :] + _dot(y_cat, wo_ref[...]))
    h = [_rms(a, g2_ref[...]).astype(BF16) for a in acc]
    cols = D_FF // FF_SPLIT
    for c in range(FF_SPLIT):
        for i in range(OUT_PARTS):
            z = _dot(h[i], w1_ref[:, c * cols:(c + 1) * cols])
            z = jnp.square(jnp.maximum(z, 0.0)).astype(BF16)
            acc[i] = acc[i] + _dot(z, w2_ref[c * cols:(c + 1) * cols, :])
    for i, rows in enumerate(parts):
        o_ref[rows, :] = _rms(acc[i], gf_ref[...])


def _mix_out(x2, yr, yc, wo, g2, w1, w2, gf):
    n = x2.shape[0]
    tm = TM_OUT
    tok = lambda w: pl.BlockSpec((tm, w), lambda i: (i, 0))
    full = lambda a: pl.BlockSpec(a.shape, lambda i: (0,) * a.ndim, pipeline_mode=pl.Buffered(1))
    consts = (wo, g2, w1, w2, gf)
    return pl.pallas_call(
        _mix_out_body,
        grid=(n // tm,),
        in_specs=[tok(D_MODEL), tok(D_RWKV), tok(D_CONV)] + [full(a) for a in consts],
        out_specs=tok(D_MODEL),
        out_shape=jax.ShapeDtypeStruct((n, D_MODEL), F32),
        compiler_params=pltpu.CompilerParams(
            dimension_semantics=("arbitrary",), vmem_limit_bytes=VMEM_LIMIT),
        name="mix_out",
    )(x2, yr, yc, *consts)


def kernel(x, norm_mix_g, w_in, shift_mu, w0, w_decay_up, a0, w_aaa_up, w_gate_up, k_k, k_a, r_k, ln_x_g, ln_x_b, conv_w, conv_b, conv_ln_g, conv_ln_b, w_out, norm_mlp_g, w_ff1, w_ff2, norm_final_g):
    batch, seq, _ = x.shape
    assert norm_mix_g.shape[0] == 1, "single layer"
    assert seq % TM_IN == 0 and seq % (REC_CHUNKS * CHUNK) == 0 and batch % REC_BATCH == 0 and (batch * seq) % TM_OUT == 0
    row = lambda a: a.reshape(1, -1)
    x2 = x.reshape(batch * seq, D_MODEL)

    zer = jnp.zeros((D_DECAY_LORA, D_RWKV), F32)
    wwa = jnp.concatenate(
        [jnp.concatenate([w_decay_up[0], zer], axis=1),
         jnp.concatenate([zer, w_aaa_up[0]], axis=1)], axis=0).astype(BF16)
    hid = jnp.arange(D_RWKV) // HEAD
    ones = (hid[:, None] == hid[None, :]).astype(BF16)

    r, lw, k, v, kk, b, g, yc = _mix_in(
        x2, row(norm_mix_g[0]), w_in[0].astype(BF16), row(shift_mu[0]), row(w0[0]), wwa,
        row(a0[0]), w_gate_up[0].astype(BF16), row(k_k[0]), row(k_a[0]), ones,
        jnp.broadcast_to(conv_w[0][:, None, :], (CONV_WIDTH, SUBLANES, D_CONV)),
        row(conv_b[0]), row(conv_ln_g[0]), row(conv_ln_b[0]), seq)
    yr = _rwkv_rec(r, lw, k, v, kk, b, g, row(r_k[0]), row(ln_x_g[0]), row(ln_x_b[0]),
                   ones * (1.0 / HEAD),
                   batch, seq)
    out = _mix_out(x2, yr, yc, w_out[0].astype(BF16), row(norm_mlp_g[0]),
                   w_ff1[0].astype(BF16), w_ff2[0].astype(BF16), row(norm_final_g))
    return out.reshape(batch, seq, D_MODEL)
```

```python
import functools

import jax
import jax.numpy as jnp
from jax import lax
from jax.experimental import pallas as pl
from jax.experimental.pallas import tpu as pltpu

F32 = jnp.float32
BF16 = jnp.bfloat16

D_MODEL = 1024
D_RWKV = 512
HEAD = 64
N_HEADS = D_RWKV // HEAD
D_CONV = 512
CONV_WIDTH = 31
D_DECAY_LORA = 64
D_AAA_LORA = 64
D_GATE_LORA = 128
D_FF = 4 * D_MODEL
D_RWKV_IN = 3 * D_RWKV + D_DECAY_LORA + D_AAA_LORA + D_GATE_LORA
D_IN = D_RWKV_IN + 2 * D_CONV
RMS_EPS = 1e-6
GN_EPS = 64e-5
LN_EPS = 1e-5
KK_NORM_FLOOR = 1e-12
DECAY_SCALE = 0.6065306597126334
LOG2_E = 1.4426950408889634

LANES = 128
SUBLANES = 8
CHUNK = 64
REC_CHUNKS = 2
REC_BATCH = 4
CONV_HIST = 40
CONV_ROWS = 16
TM_IN = 512
TM_OUT = 1024
FF_SPLIT = 4
OUT_PARTS = 4
VMEM_LIMIT = 56 * 1024 * 1024


def _sigmoid(z):
    return 1.0 / (1.0 + jnp.exp2(z * -LOG2_E))


def _dot(a, b):
    return jnp.dot(a, b, preferred_element_type=F32)


def _dot_nt(a, b):
    return lax.dot_general(a, b, (((1,), (1,)), ((), ())), preferred_element_type=F32)


def _mix_in_body(tiles_per_seq,
                 x_ref, g_ref, win_ref, mu_ref, w0_ref, wwa_ref, a0_ref, wg_ref,
                 kkp_ref, ka_ref, ones_ref, cw_ref, cb_ref, clg_ref, clb_ref,
                 r_out, lw_out, k_out, v_out, kk_out, b_out, g_out, yc_out,
                 carry_ref, hbuf_ref, odd_ref):
    tm = x_ref.shape[0]

    @pl.when(pl.program_id(0) % tiles_per_seq == 0)
    def _():
        carry_ref[...] = jnp.zeros_like(carry_ref)
        hbuf_ref[0, 0:CONV_HIST, :] = jnp.zeros((CONV_HIST, D_CONV), F32)

    x = x_ref[...]
    ms = jnp.mean(x * x, axis=-1, keepdims=True)
    h = (x * lax.rsqrt(ms + RMS_EPS) * g_ref[...]).astype(BF16)
    p = _dot(h, win_ref[...])

    pr = p[:, :D_RWKV_IN]
    rolled = pltpu.roll(pr, 1, axis=0)
    row = lax.broadcasted_iota(jnp.int32, (SUBLANES, D_RWKV_IN), 0)
    last = carry_ref[SUBLANES - 1:SUBLANES, :]
    first = jnp.where(row == 0, last, rolled[:SUBLANES])
    prev = jnp.concatenate([first, rolled[SUBLANES:]], axis=0)
    carry_ref[...] = pr[tm - SUBLANES:tm, :]
    pm = pr + (prev - pr) * mu_ref[...]

    r = pm[:, 0:D_RWKV]
    k = pm[:, D_RWKV:2 * D_RWKV]
    v = pm[:, 2 * D_RWKV:3 * D_RWKV]
    lora_in = pm[:, 3 * D_RWKV:3 * D_RWKV + LANES]
    g_down = pm[:, 3 * D_RWKV + LANES:D_RWKV_IN]

    lane = lax.broadcasted_iota(jnp.int32, lora_in.shape, 1)
    lora_act = jnp.where(lane < D_DECAY_LORA, jnp.tanh(lora_in), lora_in).astype(BF16)
    wa = _dot(lora_act, wwa_ref[...])
    lw_out[...] = -DECAY_SCALE * _sigmoid(w0_ref[...] + wa[:, :D_RWKV])
    a_sig = _sigmoid(a0_ref[...] + wa[:, D_RWKV:])
    g_out[...] = _dot(_sigmoid(g_down).astype(BF16), wg_ref[...])

    kk = k * kkp_ref[...]
    ss = _dot((kk * kk).astype(BF16), ones_ref[...])
    kk = kk * lax.rsqrt(jnp.maximum(ss, KK_NORM_FLOOR * KK_NORM_FLOOR))
    r_out[...] = r
    ka = ka_ref[...]
    k_out[...] = k * (a_sig * ka + (1.0 - ka))
    v_out[...] = v
    kk_out[...] = kk
    b_out[...] = kk * a_sig

    u1 = p[:, D_RWKV_IN:D_RWKV_IN + D_CONV]
    u2 = p[:, D_RWKV_IN + D_CONV:D_IN]
    hbuf_ref[0, CONV_HIST:CONV_HIST + tm, :] = u1 * _sigmoid(u2)
    span = tm + CONV_HIST - SUBLANES
    groups = (CONV_HIST + tm) // SUBLANES
    glu_rows = hbuf_ref[0].reshape(groups, SUBLANES, D_CONV)
    sub = lax.broadcasted_iota(jnp.int32, (groups - 1, SUBLANES, D_CONV), 1)
    for s in range(2, SUBLANES, 2):
        mixed = jnp.where(sub >= s, glu_rows[:-1], glu_rows[1:])
        hbuf_ref[s // 2, 0:span, :] = pltpu.roll(mixed, SUBLANES - s, axis=1).reshape(span, D_CONV)
    base = CONV_HIST - (CONV_WIDTH - 1)

    def taps(acc, first, rows, parity):
        for j in range(parity, CONV_WIDTH, 2):
            off = base + j + parity
            s = off % SUBLANES
            wj = jnp.concatenate([cw_ref[j]] * (rows // SUBLANES), axis=0)
            term = wj * hbuf_ref[s // 2, pl.ds(first + off - s, rows), :]
            acc = term if acc is None else acc + term
        return acc

    for q0 in range(0, tm + SUBLANES, CONV_ROWS):
        rows = min(CONV_ROWS, tm + SUBLANES - q0)
        odd_ref[q0:q0 + rows, :] = taps(None, q0 - SUBLANES, rows, 1)
    blk_groups = CONV_ROWS // SUBLANES
    sub_blk = lax.broadcasted_iota(jnp.int32, (blk_groups, SUBLANES, D_CONV), 1)
    for r0 in range(0, tm, CONV_ROWS):
        acc = taps(jnp.broadcast_to(cb_ref[...], (CONV_ROWS, D_CONV)), r0, CONV_ROWS, 0)
        og = odd_ref[r0:r0 + CONV_ROWS + SUBLANES, :].reshape(blk_groups + 1, SUBLANES, D_CONV)
        og = jnp.where(sub_blk >= SUBLANES - 1, og[:-1], og[1:])
        acc = acc + pltpu.roll(og, 1, axis=1).reshape(CONV_ROWS, D_CONV)
        mu = jnp.mean(acc, axis=-1, keepdims=True)
        d = acc - mu
        var = jnp.mean(d * d, axis=-1, keepdims=True)
        hn = d * lax.rsqrt(var + LN_EPS) * clg_ref[...] + clb_ref[...]
        yc_out[r0:r0 + CONV_ROWS, :] = (hn * _sigmoid(hn)).astype(BF16)
    hbuf_ref[0, 0:CONV_HIST, :] = hbuf_ref[0, tm:tm + CONV_HIST, :]


def _mix_in(x2, g, win, mu, w0, wwa, a0, wg, kkp, ka, ones, cw, cb, clg, clb, seq):
    n = x2.shape[0]
    tm = TM_IN
    full = lambda a: pl.BlockSpec(a.shape, lambda i: (0,) * a.ndim, pipeline_mode=pl.Buffered(1))
    tok = lambda w: pl.BlockSpec((tm, w), lambda i: (i, 0))
    consts = (g, win, mu, w0, wwa, a0, wg, kkp, ka, ones, cw, cb, clg, clb)
    out_f32 = jax.ShapeDtypeStruct((n, D_RWKV), F32)
    return pl.pallas_call(
        functools.partial(_mix_in_body, seq // tm),
        grid=(n // tm,),
        in_specs=[tok(D_MODEL)] + [full(a) for a in consts],
        out_specs=[tok(D_RWKV)] * 7 + [tok(D_CONV)],
        out_shape=[out_f32] * 7 + [jax.ShapeDtypeStruct((n, D_CONV), BF16)],
        scratch_shapes=[pltpu.VMEM((SUBLANES, D_RWKV_IN), F32),
                        pltpu.VMEM((SUBLANES // 2, CONV_HIST + tm, D_CONV), F32),
                        pltpu.VMEM((tm + SUBLANES, D_CONV), F32)],
        compiler_params=pltpu.CompilerParams(
            dimension_semantics=("arbitrary",), vmem_limit_bytes=VMEM_LIMIT),
        name="mix_in",
    )(x2, *consts)


def _rwkv_rec_body(r_ref, lw_ref, k_ref, v_ref, kk_ref, b_ref, g_ref,
                   rk_ref, lng_ref, lnb_ref, mean_ref,
                   o_ref, st_ref, y_ref):
    L = CHUNK
    nb, rc = r_ref.shape[1], r_ref.shape[3]
    cpb = rc // L
    nch = nb * cpb

    def chunk(ref, vc):
        return ref[0, vc // cpb, 0, pl.ds((vc % cpb) * L, L), :]

    def whole(ref):
        return jnp.concatenate([ref[0, g, 0] for g in range(nb)], axis=0)

    @pl.when(pl.program_id(1) == 0)
    def _():
        st_ref[...] = jnp.zeros_like(st_ref)

    ri = lax.broadcasted_iota(jnp.int32, (L, L), 0)
    ci = lax.broadcasted_iota(jnp.int32, (L, L), 1)
    tri = jnp.where(ri >= ci, 1.0, 0.0).astype(BF16)
    last_row = lax.broadcasted_iota(jnp.int32, (L, D_RWKV), 0) == L - 1
    lane = lax.broadcasted_iota(jnp.int32, (2 * L, LANES), 1)
    rowi = lax.broadcasted_iota(jnp.int32, (2 * L, LANES), 0)
    head_mask2 = (lane < HEAD, lane >= HEAD)
    t_idx = rowi & (L - 1)
    j_idx = lane & (L - 1)
    tri_mask = (j_idx < t_idx) | ((rowi >= L) & (j_idx == t_idx))
    lane1 = lax.broadcasted_iota(jnp.int32, (L, LANES), 1)
    row1 = lax.broadcasted_iota(jnp.int32, (L, LANES), 0)
    head_mask1 = (lane1 < HEAD, lane1 >= HEAD)
    diag_mask = row1 == lane1
    eye_pair = jnp.where(row1 == (lane1 & (L - 1)), 1.0, 0.0)
    zeros_b = jnp.zeros((L, LANES), BF16)
    pairs = range(N_HEADS // 2)
    sls = [slice(pr * LANES, (pr + 1) * LANES) for pr in pairs]

    q_p, c_p, d_t, q_roll, v_p, v_roll, pe_p, pe_roll = ({} for _ in range(8))
    for cc in range(nch):
        lw = chunk(lw_ref, cc)
        lw_hi = lw.astype(BF16)
        lw_lo = (lw - lw_hi.astype(F32)).astype(BF16)
        lp = _dot(tri, lw_hi) + _dot(tri, lw_lo)
        lp_end = jnp.sum(jnp.where(last_row, lp, 0.0), axis=0, keepdims=True)
        p_end = jnp.exp(lp_end)
        k = chunk(k_ref, cc)
        v = chunk(v_ref, cc)
        b = chunk(b_ref, cc)
        e_neg = jnp.exp2(lp * -LOG2_E)
        e_end = p_end * e_neg
        at = -(chunk(kk_ref, cc) * jnp.exp(lp - lw))
        rt = chunk(r_ref, cc) * jnp.exp(lp)
        bt = b * e_neg
        kt = k * e_neg
        bp = b * e_end
        kp = k * e_end
        for pr in pairs:
            sl = sls[pr]
            u = (cc, pr)
            q_p[u] = jnp.concatenate([at[:, sl], rt[:, sl]], axis=0)
            c_p[u] = jnp.concatenate([bt[:, sl], kt[:, sl]], axis=0).astype(BF16)
            d_t[u] = jnp.concatenate([bp[:, sl], kp[:, sl]], axis=0).T.astype(BF16)
            q_roll[u] = pltpu.roll(q_p[u], HEAD, axis=1)
            v_p[u] = v[:, sl]
            v_roll[u] = pltpu.roll(v_p[u], HEAD, axis=1)
            pe_p[u] = p_end[:, sl]
            pe_roll[u] = pltpu.roll(pe_p[u], HEAD, axis=1)

    units = [(cc, pr, hh) for cc in range(nch) for pr in pairs for hh in range(2)]
    a_top, a_bot, q_lo, v_hi, xw, npow = [], [], [], [], [], []
    zeros_c = jnp.zeros((2 * L, LANES), BF16)
    for cc, pr, hh in units:
        u = (cc, pr)
        if hh == 0:
            c_heads = jnp.concatenate([jnp.where(head_mask2[0], c_p[u], zeros_c),
                                       jnp.where(head_mask2[1], c_p[u], zeros_c)], axis=0)
            a_pair = _dot_nt(q_p[u].astype(BF16), c_heads)
        a_all = a_pair[:, hh * 2 * L:(hh + 1) * 2 * L]
        a_all = jnp.where(tri_mask, a_all, 0.0).astype(BF16)
        a_top.append(a_all[:L])
        a_bot.append(a_all[L:])
        q_lo.append(jnp.where(head_mask2[0], q_p[u] if hh == 0 else q_roll[u], 0.0))
        v_hi.append(jnp.where(head_mask1[1], v_p[u] if hh == 1 else v_roll[u], 0.0).astype(BF16))
    for i in range(len(units)):
        zv = jnp.concatenate([zeros_b, v_hi[i]], axis=0)
        xw.append(q_lo[i][:L] + _dot(a_top[i], zv))
        npow.append(a_top[i])
    for it in range(6):
        lo = {4: L // 4, 5: L // 2}.get(it, 0)
        for i in range(len(units)):
            xb = xw[i].astype(BF16)
            if it < 4:
                rhs = jnp.concatenate(
                    [jnp.concatenate([xb, npow[i]], axis=1),
                     jnp.zeros((L, 2 * LANES), BF16)], axis=0)
                res = _dot(npow[i], rhs)
                xw[i] = xw[i] + res[:, :LANES]
                npow[i] = res[:, LANES:].astype(BF16)
            elif it == 4:
                keep = L - lo
                rhs = jnp.concatenate(
                    [jnp.concatenate([xb[:keep], npow[i][:keep]], axis=1),
                     jnp.zeros((2 * L - keep, 2 * LANES), BF16)], axis=0)
                res = _dot(npow[i][lo:], rhs)
                xw[i] = jnp.concatenate([xw[i][:lo], xw[i][lo:] + res[:, :LANES]], axis=0)
                npow[i] = res[L // 2 - lo:, LANES:].astype(BF16)
            else:
                keep = L - lo
                rhs = jnp.concatenate([xb[:keep], jnp.zeros((2 * L - keep, LANES), BF16)], axis=0)
                res = _dot(npow[i], rhs)
                xw[i] = jnp.concatenate([xw[i][:lo], xw[i][lo:] + res], axis=0)
    lhs2 = {}
    for i, (cc, pr, hh) in enumerate(units):
        u = (cc, pr)
        m2 = jnp.concatenate([xw[i].astype(BF16), v_hi[i]], axis=0)
        both = _dot(jnp.concatenate([a_bot[i], d_t[u][hh * HEAD:(hh + 1) * HEAD, :]], axis=0), m2)
        ry = both[:L] + q_lo[i][L:]
        gh = both[L:]
        gh = gh + jnp.where(diag_mask, pe_p[u] if hh == 0 else pe_roll[u], 0.0)
        lhs2[(cc, pr, hh)] = jnp.concatenate([ry, gh], axis=0).astype(BF16)

    eye_rows = [jnp.where(head_mask1[hh], eye_pair, 0.0).astype(BF16) for hh in range(2)]
    n_pairs = N_HEADS // 2
    st = [st_ref[i] for i in range(nb * n_pairs)]
    for c2 in range(cpb):
        for g in range(nb):
            cc = g * cpb + c2
            for pr in pairs:
                si = g * n_pairs + pr
                rhs2 = jnp.concatenate(
                    [jnp.where(head_mask1[0], st[si], 0.0).astype(BF16), eye_rows[0],
                     jnp.where(head_mask1[1], st[si], 0.0).astype(BF16), eye_rows[1]], axis=0)
                out = _dot(jnp.concatenate([lhs2[(cc, pr, 0)], lhs2[(cc, pr, 1)]], axis=1), rhs2)
                y_ref[pl.ds(cc * L, L), sls[pr]] = out[:L]
                st[si] = out[L:]
    for i in range(nb * n_pairs):
        st_ref[i] = st[i]

    y = y_ref[...]
    half = D_RWKV // 2
    mean_half = mean_ref[0:half, 0:half]

    def head_mean(t):
        tb = t.astype(BF16)
        return jnp.concatenate([_dot(tb[:, :half], mean_half), _dot(tb[:, half:], mean_half)], axis=1)

    mu = head_mean(y)
    d = y - mu
    var = head_mean(d * d)
    yn = d * lax.rsqrt(var + GN_EPS) * lng_ref[...] + lnb_ref[...]
    bonus = head_mean(whole(r_ref) * whole(k_ref) * (rk_ref[...] * float(HEAD))) * whole(v_ref)
    out_all = ((yn + bonus) * whole(g_ref)).astype(BF16)
    for g in range(nb):
        o_ref[0, g, 0] = out_all[g * rc:(g + 1) * rc]


def _rwkv_rec(r, lw, k, v, kk, b, g, rk, lng, lnb, head_mean, batch, seq):
    rc = REC_CHUNKS * CHUNK
    nb = REC_BATCH
    shape5 = (batch // nb, nb, seq // rc, rc, D_RWKV)
    tok = pl.BlockSpec((1, nb, 1, rc, D_RWKV), lambda bi, ci: (bi, 0, ci, 0, 0))
    full = lambda a: pl.BlockSpec(a.shape, lambda bi, ci: (0,) * a.ndim)
    consts = (rk, lng, lnb, head_mean)
    out = pl.pallas_call(
        _rwkv_rec_body,
        grid=(batch // nb, seq // rc),
        in_specs=[tok] * 7 + [full(a) for a in consts],
        out_specs=tok,
        out_shape=jax.ShapeDtypeStruct(shape5, BF16),
        scratch_shapes=[pltpu.VMEM((nb * (N_HEADS // 2), HEAD, LANES), F32),
                        pltpu.VMEM((nb * rc, D_RWKV), F32)],
        compiler_params=pltpu.CompilerParams(
            dimension_semantics=("arbitrary", "arbitrary"), vmem_limit_bytes=VMEM_LIMIT),
        name="rwkv_rec",
    )(*[a.reshape(shape5) for a in (r, lw, k, v, kk, b, g)], *consts)
    return out.reshape(batch * seq, D_RWKV)


def _rms(x, g):
    ms = jnp.mean(x * x, axis=-1, keepdims=True)
    return x * lax.rsqrt(ms + RMS_EPS) * g


def _mix_out_body(x_ref, yr_ref, yc_ref, wo_ref, g2_ref, w1_ref, w2_ref, gf_ref, o_ref):
    part = x_ref.shape[0] // OUT_PARTS
    parts = [slice(i * part, (i + 1) * part) for i in range(OUT_PARTS)]
    acc = []
    for rows in parts:
        y_cat = jnp.concatenate([yr_ref[rows, :], yc_ref[rows, :]], axis=1)
        acc.append(x_ref[rows, :] + _dot(y_cat, wo_ref[...]))
    h = [_rms(a, g2_ref[...]).astype(BF16) for a in acc]
    cols = D_FF // FF_SPLIT
    for c in range(FF_SPLIT):
        for i in range(OUT_PARTS):
            z = _dot(h[i], w1_ref[:, c * cols:(c + 1) * cols])
            z = jnp.square(jnp.maximum(z, 0.0)).astype(BF16)
            acc[i] = acc[i] + _dot(z, w2_ref[c * cols:(c + 1) * cols, :])
    for i, rows in enumerate(parts):
        o_ref[rows, :] = _rms(acc[i], gf_ref[...])


def _mix_out(x2, yr, yc, wo, g2, w1, w2, gf):
    n = x2.shape[0]
    tm = TM_OUT
    tok = lambda w: pl.BlockSpec((tm, w), lambda i: (i, 0))
    full = lambda a: pl.BlockSpec(a.shape, lambda i: (0,) * a.ndim, pipeline_mode=pl.Buffered(1))
    consts = (wo, g2, w1, w2, gf)
    return pl.pallas_call(
        _mix_out_body,
        grid=(n // tm,),
        in_specs=[tok(D_MODEL), tok(D_RWKV), tok(D_CONV)] + [full(a) for a in consts],
        out_specs=tok(D_MODEL),
        out_shape=jax.ShapeDtypeStruct((n, D_MODEL), F32),
        compiler_params=pltpu.CompilerParams(
            dimension_semantics=("arbitrary",), vmem_limit_bytes=VMEM_LIMIT),
        name="mix_out",
    )(x2, yr, yc, *consts)


def kernel(x, norm_mix_g, w_in, shift_mu, w0, w_decay_up, a0, w_aaa_up, w_gate_up, k_k, k_a, r_k, ln_x_g, ln_x_b, conv_w, conv_b, conv_ln_g, conv_ln_b, w_out, norm_mlp_g, w_ff1, w_ff2, norm_final_g):
    batch, seq, _ = x.shape
    assert norm_mix_g.shape[0] == 1, "single layer"
    assert seq % TM_IN == 0 and seq % (REC_CHUNKS * CHUNK) == 0 and batch % REC_BATCH == 0 and (batch * seq) % TM_OUT == 0
    row = lambda a: a.reshape(1, -1)
    x2 = x.reshape(batch * seq, D_MODEL)

    zer = jnp.zeros((D_DECAY_LORA, D_RWKV), F32)
    wwa = jnp.concatenate(
        [jnp.concatenate([w_decay_up[0], zer], axis=1),
         jnp.concatenate([zer, w_aaa_up[0]], axis=1)], axis=0).astype(BF16)
    hid = jnp.arange(D_RWKV) // HEAD
    ones = (hid[:, None] == hid[None, :]).astype(BF16)

    r, lw, k, v, kk, b, g, yc = _mix_in(
        x2, row(norm_mix_g[0]), w_in[0].astype(BF16), row(shift_mu[0]), row(w0[0]), wwa,
        row(a0[0]), w_gate_up[0].astype(BF16), row(k_k[0]), row(k_a[0]), ones,
        jnp.broadcast_to(conv_w[0][:, None, :], (CONV_WIDTH, SUBLANES, D_CONV)),
        row(conv_b[0]), row(conv_ln_g[0]), row(conv_ln_b[0]), seq)
    yr = _rwkv_rec(r, lw, k, v, kk, b, g, row(r_k[0]), row(ln_x_g[0]), row(ln_x_b[0]),
                   ones * (1.0 / HEAD),
                   batch, seq)
    out = _mix_out(x2, yr, yc, w_out[0].astype(BF16), row(norm_mlp_g[0]),
                   w_ff1[0].astype(BF16), w_ff2[0].astype(BF16), row(norm_final_g))
    return out.reshape(batch, seq, D_MODEL)
```
